```python
import jax, jax.numpy as jnp
from jax import lax
import numpy as np

D_MODEL = 4096
BATCH = 2
SEQ = 8192
DEPTH = 2

CTX_LEN = 256
GRID_W = 64
MIX_WIDTH = D_MODEL
HEAD_DIM = 128
NA_HEADS = (MIX_WIDTH // 2) // HEAD_DIM
NA_WIDTH = NA_HEADS * HEAD_DIM
CONV_WIDTH = MIX_WIDTH // 4
SG_WIDTH = MIX_WIDTH - NA_WIDTH - CONV_WIDTH
SG_GROUPS = 8
SG_GROUP_DIM = SG_WIDTH // SG_GROUPS
CHUNK = 128
NA_KH_MAX = 8
NA_KW = 16
CONV_K = 3
N_EXPERTS = 16
EXPERT_FF = D_MODEL // 4
CAPACITY_FACTOR = 2
N_MOD = 6
EPS = 1e-6
IN_COLS = 3 * NA_WIDTH + 3 * CONV_WIDTH + 2 * SG_WIDTH
SPLITS = (NA_WIDTH, 2 * NA_WIDTH, 3 * NA_WIDTH,
          3 * NA_WIDTH + CONV_WIDTH, 3 * NA_WIDTH + 2 * CONV_WIDTH, 3 * NA_WIDTH + 3 * CONV_WIDTH,
          3 * NA_WIDTH + 3 * CONV_WIDTH + SG_WIDTH)

kernel_name = 'hybrid_natten_conv_gmlp_ec_moe_dit'


def rms_norm(x, g):
    xf = x.astype(jnp.float32)
    y = xf * lax.rsqrt(jnp.mean(xf * xf, axis=-1, keepdims=True) + EPS)
    return (y * g.astype(jnp.float32)).astype(x.dtype)


def adaln(cvec, w, b):
    return jax.nn.silu(cvec) @ w + b


def modulate(h, shift, scale):
    return h * (1 + scale) + shift


def to_heads(t, gain=None):
    t = t.reshape(*t.shape[:-1], NA_HEADS, HEAD_DIM)
    return t if gain is None else rms_norm(t, gain)


def neighbourhood_attention(q, k, v, k_ctx, v_ctx, rpb):
    B, N, H, dh = q.shape
    rows = N // GRID_W
    kh = min(NA_KH_MAX, rows)
    scale = dh ** -0.5
    qg = q.reshape(B, rows, GRID_W, H, dh)
    kg = k.reshape(B, rows, GRID_W, H, dh)
    vg = v.reshape(B, rows, GRID_W, H, dh)
    cols = jnp.arange(GRID_W)
    col_start = jnp.clip(cols - NA_KW // 2, 0, GRID_W - NA_KW)
    col_valid = (cols[None, :] >= col_start[:, None]) & (cols[None, :] < col_start[:, None] + NA_KW)
    mask = jnp.broadcast_to(col_valid[:, None, :], (GRID_W, kh, GRID_W)).reshape(GRID_W, kh * GRID_W)
    dc_idx = jnp.clip(cols[None, :] - cols[:, None] + NA_KW - 1, 0, 2 * NA_KW - 2)
    rpb32 = rpb.astype(jnp.float32)

    def row_block(r):
        start = jnp.clip(r - kh // 2, 0, rows - kh)
        q_r = lax.dynamic_index_in_dim(qg, r, axis=1, keepdims=False)
        k_r = lax.dynamic_slice_in_dim(kg, start, kh, axis=1).reshape(B, kh * GRID_W, H, dh)
        v_r = lax.dynamic_slice_in_dim(vg, start, kh, axis=1).reshape(B, kh * GRID_W, H, dh)
        dr_idx = start + jnp.arange(kh) - r + NA_KH_MAX - 1
        bias = rpb32[:, dr_idx[:, None, None], dc_idx[None, :, :]]
        bias = bias.transpose(0, 2, 1, 3).reshape(H, GRID_W, kh * GRID_W)
        s_loc = jnp.einsum('bqhd,bkhd->bhqk', q_r, k_r).astype(jnp.float32) * scale + bias[None]
        s_loc = jnp.where(mask, s_loc, -jnp.inf)
        s_ctx = jnp.einsum('bqhd,bkhd->bhqk', q_r, k_ctx).astype(jnp.float32) * scale
        p = jax.nn.softmax(jnp.concatenate([s_loc, s_ctx], axis=-1), axis=-1).astype(v.dtype)
        n_loc = kh * GRID_W
        return (jnp.einsum('bhqk,bkhd->bqhd', p[..., :n_loc], v_r)
                + jnp.einsum('bhqk,bkhd->bqhd', p[..., n_loc:], v_ctx))

    out = lax.map(row_block, jnp.arange(rows))
    return out.transpose(1, 0, 2, 3, 4).reshape(B, N, H * dh)


def context_attention(q, k, v):
    B, L, H, dh = q.shape
    s = jnp.einsum('bqhd,bkhd->bhqk', q, k).astype(jnp.float32) * (dh ** -0.5)
    p = jax.nn.softmax(s, axis=-1).astype(v.dtype)
    return jnp.einsum('bhqk,bkhd->bqhd', p, v).reshape(B, L, H * dh)


def gated_short_conv(xin, gate_b, gate_c, conv_w):
    ch = xin.shape[-1]
    z = gate_c * xin
    z = lax.conv_general_dilated(z, conv_w[:, None, :].astype(z.dtype), window_strides=(1,),
                                 padding=((CONV_K // 2, CONV_K // 2),),
                                 dimension_numbers=('NWC', 'WIO', 'NWC'), feature_group_count=ch)
    return gate_b * z


def spatial_gating(u, v, sg_norm, sg_w, sg_b):
    B, L, _ = u.shape
    vn = rms_norm(v, sg_norm).reshape(B, L // CHUNK, CHUNK, SG_GROUPS, SG_GROUP_DIM)
    mixed = jnp.einsum('gts,bnsgd->bntgd', sg_w, vn) + sg_b.T[None, None, :, :, None]
    return u * mixed.reshape(B, L, SG_WIDTH)


def expert_choice_ffn(h, w_router, w_gate, w_up, w_down):
    B, N, D = h.shape
    cap = CAPACITY_FACTOR * N // N_EXPERTS
    aff = jax.nn.softmax((h @ w_router).astype(jnp.float32), axis=-1)
    gate, idx = lax.top_k(aff.transpose(0, 2, 1), cap)
    bidx = jnp.arange(B)[:, None, None]
    hs = h[bidx, idx]
    a = jax.nn.silu(jnp.einsum('becd,edf->becf', hs, w_gate)) * jnp.einsum('becd,edf->becf', hs, w_up)
    y = jnp.einsum('becf,efd->becd', a, w_down) * gate[..., None].astype(h.dtype)
    return jnp.zeros_like(h).at[bidx, idx].add(y)


def mixer_concat(attn_out, xin, gb, gc, su, sv, conv_w, sg_norm, sg_w, sg_b):
    return jnp.concatenate([attn_out, gated_short_conv(xin, gb, gc, conv_w),
                            spatial_gating(su, sv, sg_norm, sg_w, sg_b)], axis=-1)


def hybrid_layer(x, xc, c, c_ctx, w_ada, b_ada, norm1, norm2, w_in, q_norm, k_norm, rpb, conv_w,
                 sg_norm, sg_w, sg_b, w_out, w_router, w_gate, w_up, w_down, update_ctx):
    D = x.shape[-1]
    sh1, sc1, g1, sh2, sc2, g2 = jnp.split(adaln(c, w_ada, b_ada)[:, None, :], N_MOD, axis=-1)

    if update_ctx:
        csh1, csc1, cg1, csh2, csc2, cg2 = jnp.split(adaln(c_ctx, w_ada, b_ada), N_MOD, axis=-1)
        hc = modulate(rms_norm(xc, norm1), csh1, csc1)
        qc_, kc_, vc_, xin_c, gb_c, gc_c, su_c, sv_c = jnp.split(hc @ w_in, SPLITS, axis=-1)
    else:
        csh1, csc1 = jnp.split(adaln(c_ctx, w_ada[:, :2 * D], b_ada[:2 * D]), 2, axis=-1)
        hc = modulate(rms_norm(xc, norm1), csh1, csc1)
        kc_, vc_ = jnp.split(hc @ w_in[:, NA_WIDTH:3 * NA_WIDTH], 2, axis=-1)
    k_ctx = to_heads(kc_, k_norm)
    v_ctx = to_heads(vc_)

    h = modulate(rms_norm(x, norm1), sh1, sc1)
    q_, k_, v_, xin, gb, gc, su, sv = jnp.split(h @ w_in, SPLITS, axis=-1)
    a = neighbourhood_attention(to_heads(q_, q_norm), to_heads(k_, k_norm), to_heads(v_), k_ctx, v_ctx, rpb)
    m = mixer_concat(a, xin, gb, gc, su, sv, conv_w, sg_norm, sg_w, sg_b)
    x = x + g1 * (m @ w_out)
    h2 = modulate(rms_norm(x, norm2), sh2, sc2)
    x = x + g2 * expert_choice_ffn(h2, w_router, w_gate, w_up, w_down)

    if update_ctx:
        a_c = context_attention(to_heads(qc_, q_norm), k_ctx, v_ctx)
        m_c = mixer_concat(a_c, xin_c, gb_c, gc_c, su_c, sv_c, conv_w, sg_norm, sg_w, sg_b)
        xc = xc + cg1 * (m_c @ w_out)
        hc2 = modulate(rms_norm(xc, norm2), csh2, csc2)
        xc = xc + cg2 * expert_choice_ffn(hc2, w_router, w_gate, w_up, w_down)
    return x, xc


def setup_inputs(seed: int = 0) -> dict:
    key = jax.random.key(seed)
    ks = jax.random.split(key, 24)
    D = D_MODEL
    nrm = lambda k, shape, s: jax.random.normal(k, shape, jnp.float32) * s
    return {
        'x': nrm(ks[0], (BATCH, SEQ, D), 1.0),
        'c': nrm(ks[1], (BATCH, D), 1.0),
        'ctx': nrm(ks[2], (BATCH, CTX_LEN, D), 1.0),
        'c_ctx': nrm(ks[3], (D,), 1.0),
        'w_ada': nrm(ks[4], (DEPTH, D, N_MOD * D), 0.5 * D ** -0.5),
        'b_ada': nrm(ks[5], (DEPTH, N_MOD * D), 0.01),
        'norm1': 1.0 + nrm(ks[6], (DEPTH, D), 0.02),
        'norm2': 1.0 + nrm(ks[7], (DEPTH, D), 0.02),
        'w_in': nrm(ks[8], (DEPTH, D, IN_COLS), D ** -0.5),
        'q_norm': 1.0 + nrm(ks[9], (DEPTH, HEAD_DIM), 0.02),
        'k_norm': 1.0 + nrm(ks[10], (DEPTH, HEAD_DIM), 0.02),
        'rpb': nrm(ks[11], (DEPTH, NA_HEADS, 2 * NA_KH_MAX - 1, 2 * NA_KW - 1), 0.1),
        'conv_w': nrm(ks[12], (DEPTH, CONV_K, CONV_WIDTH), CONV_K ** -0.5),
        'sg_norm': 1.0 + nrm(ks[13], (DEPTH, SG_WIDTH), 0.02),
        'sg_w': nrm(ks[14], (DEPTH, SG_GROUPS, CHUNK, CHUNK), CHUNK ** -0.5),
        'sg_b': nrm(ks[15], (DEPTH, SG_GROUPS, CHUNK), 0.1),
        'w_out': nrm(ks[16], (DEPTH, MIX_WIDTH, D), MIX_WIDTH ** -0.5),
        'w_router': nrm(ks[17], (DEPTH, D, N_EXPERTS), D ** -0.5),
        'w_gate': nrm(ks[18], (DEPTH, N_EXPERTS, D, EXPERT_FF), D ** -0.5),
        'w_up': nrm(ks[19], (DEPTH, N_EXPERTS, D, EXPERT_FF), D ** -0.5),
        'w_down': nrm(ks[20], (DEPTH, N_EXPERTS, EXPERT_FF, D), EXPERT_FF ** -0.5),
    }


def reference(x, c, ctx, c_ctx, w_ada, b_ada, norm1, norm2, w_in, q_norm, k_norm, rpb, conv_w,
              sg_norm, sg_w, sg_b, w_out, w_router, w_gate, w_up, w_down):
    xc = ctx
    for layer in range(DEPTH):
        x, xc = hybrid_layer(x, xc, c, c_ctx, w_ada[layer], b_ada[layer], norm1[layer], norm2[layer],
                             w_in[layer], q_norm[layer], k_norm[layer], rpb[layer], conv_w[layer],
                             sg_norm[layer], sg_w[layer], sg_b[layer], w_out[layer], w_router[layer],
                             w_gate[layer], w_up[layer], w_down[layer], update_ctx=layer < DEPTH - 1)
    return x
```

```python
import functools

import jax
import jax.numpy as jnp
from jax import lax
from jax.experimental import pallas as pl
from jax.experimental.pallas import tpu as pltpu

F32 = jnp.float32
BF16 = jnp.bfloat16

GRID_W = 64
HEAD_DIM = 128
NA_KH = 8
NA_KW = 16
CHUNK = 128
N_MOD = 6
CAPACITY_FACTOR = 2
EPS = 1e-6
NEG = -1e30

LANES = 128
SUBLANES = 8
VMEM_LIMIT = 56 * 1024 * 1024

ATT_ROWS = 4
ATT_KROWS = ATT_ROWS + NA_KH
ATT_HPG = 2


def _cparams(sem, vmem=VMEM_LIMIT):
    return pltpu.CompilerParams(dimension_semantics=sem, vmem_limit_bytes=vmem)


def _tile(n, pref, align):
    if n <= pref:
        return n
    t = (pref // align) * align
    while t >= align:
        if n % t == 0:
            return t
        t -= align
    raise ValueError(f"no tile for {n} (pref {pref}, align {align})")


def _dot(a, b):
    return jnp.dot(a, b, preferred_element_type=F32)


def _dot_nt(a, b):
    return lax.dot_general(a, b, (((1,), (1,)), ((), ())), preferred_element_type=F32)


def _silu(v):
    return v * jax.nn.sigmoid(v)


def _adaln_kernel(c_ref, w_ref, b_ref, o_ref):
    c = c_ref[...]
    s = _silu(c).astype(BF16)
    o_ref[0] = _dot(s, w_ref[0].astype(BF16)) + b_ref[0]


def _adaln(cvec, w_ada, b_ada):
    depth, d, cols = w_ada.shape
    rows = cvec.shape[0]
    tn = _tile(cols, 512, LANES)
    return pl.pallas_call(
        _adaln_kernel,
        out_shape=jax.ShapeDtypeStruct((depth, rows, cols), F32),
        grid=(depth, cols // tn),
        in_specs=[
            pl.BlockSpec((rows, d), lambda l, j: (0, 0)),
            pl.BlockSpec((1, d, tn), lambda l, j: (l, 0, j)),
            pl.BlockSpec((1, 1, tn), lambda l, j: (l, 0, j)),
        ],
        out_specs=pl.BlockSpec((1, rows, tn), lambda l, j: (l, 0, j)),
        compiler_params=_cparams(("parallel", "parallel")),
        name="adaln",
    )(cvec, w_ada, b_ada.reshape(depth, 1, cols))


def _att_cases(rows):
    nrb = rows // ATT_ROWS
    cases = []
    for rb in (0, min(1, nrb - 1), nrb - 1):
        r0 = rb * ATT_ROWS
        u = min(max(r0 - NA_KH // 2, 0), rows - ATT_KROWS)
        per_row = []
        for a in range(ATT_ROWS):
            r = r0 + a
            s = min(max(r - NA_KH // 2, 0), rows - NA_KH)
            per_row.append((s - u, s - r + NA_KH - 1))
        cases.append(per_row)
    return cases


def _bias_kernel(rpb_ref, o_ref, *, cases, n_dr, n_dc):
    case = pl.program_id(0)
    h = pl.program_id(1)
    qc = lax.broadcasted_iota(jnp.int32, (GRID_W, GRID_W), 0)
    kc = lax.broadcasted_iota(jnp.int32, (GRID_W, GRID_W), 1)
    cs = jnp.clip(qc - NA_KW // 2, 0, GRID_W - NA_KW)
    col_ok = (kc >= cs) & (kc < cs + NA_KW)
    dc = kc - qc + NA_KW - 1
    neg = jnp.full((GRID_W, GRID_W), NEG, F32)

    def toeplitz(dr):
        t = neg
        for i in range(n_dc):
            t = jnp.where(dc == i, rpb_ref[h, dr, i], t)
        return jnp.where(col_ok, t, neg)

    tiles = [toeplitz(dr) for dr in range(n_dr)]
    for ci, per_row in enumerate(cases):
        @pl.when(case == ci)
        def _(per_row=per_row):
            for a, (j_first, dr_first) in enumerate(per_row):
                for j in range(ATT_KROWS):
                    inside = j_first <= j < j_first + NA_KH
                    t = tiles[dr_first + (j - j_first)] if inside else neg
                    o_ref[0, 0, a * GRID_W:(a + 1) * GRID_W, j * GRID_W:(j + 1) * GRID_W] = t


def _bias_table(rpb_l, rows):
    h, n_dr, n_dc = rpb_l.shape
    cases = _att_cases(rows)
    rq, rk = ATT_ROWS * GRID_W, ATT_KROWS * GRID_W
    return pl.pallas_call(
        functools.partial(_bias_kernel, cases=cases, n_dr=n_dr, n_dc=n_dc),
        out_shape=jax.ShapeDtypeStruct((3, h, rq, rk), F32),
        grid=(3, h),
        in_specs=[pl.BlockSpec(memory_space=pltpu.SMEM)],
        out_specs=pl.BlockSpec((1, 1, rq, rk), lambda c, hh: (c, hh, 0, 0)),
        compiler_params=_cparams(("parallel", "parallel")),
        name="rpb_bias",
    )(rpb_l)


def _inproj_kernel(x_ref, sh_ref, sc_ref, g_ref, w_ref, qg_ref, kg_ref, o_ref, h_scr, *, j0, nq, tn):
    j = pl.program_id(1)

    @pl.when(j == 0)
    def _():
        xf = x_ref[...]
        ms = jnp.mean(xf * xf, axis=-1, keepdims=True)
        y = xf * lax.rsqrt(ms + EPS) * g_ref[...]
        h_scr[...] = (y * (1.0 + sc_ref[0]) + sh_ref[0]).astype(BF16)

    acc = _dot(h_scr[...], w_ref[...])
    jj = j + j0

    def head_norm(gain_ref):
        for hh in range(tn // HEAD_DIM):
            sl = slice(hh * HEAD_DIM, (hh + 1) * HEAD_DIM)
            blk = acc[:, sl]
            ms = jnp.mean(blk * blk, axis=-1, keepdims=True)
            o_ref[:, sl] = (blk * lax.rsqrt(ms + EPS) * gain_ref[...]).astype(BF16)

    @pl.when(jj < nq)
    def _():
        head_norm(qg_ref)

    @pl.when((jj >= nq) & (jj < 2 * nq))
    def _():
        head_norm(kg_ref)

    @pl.when(jj >= 2 * nq)
    def _():
        o_ref[...] = acc.astype(BF16)


def _inproj(x2d, shift, scale, gain, w_bf, q_gain, k_gain, seq_len, *, col0, ncols, na_width):
    t, d = x2d.shape
    tm = _tile(seq_len, 512, 16)
    tn = _tile(na_width // 2, 1024, HEAD_DIM)
    assert col0 % tn == 0 and ncols % tn == 0 and t % tm == 0
    nrows = shift.shape[0]
    row = (lambda i: (i * tm) // seq_len) if nrows > 1 else (lambda i: 0)
    j0 = col0 // tn
    return pl.pallas_call(
        functools.partial(_inproj_kernel, j0=j0, nq=na_width // tn, tn=tn),
        out_shape=jax.ShapeDtypeStruct((t, ncols), BF16),
        grid=(t // tm, ncols // tn),
        in_specs=[
            pl.BlockSpec((tm, d), lambda i, j: (i, 0)),
            pl.BlockSpec((1, 1, d), lambda i, j: (row(i), 0, 0)),
            pl.BlockSpec((1, 1, d), lambda i, j: (row(i), 0, 0)),
            pl.BlockSpec((1, d), lambda i, j: (0, 0)),
            pl.BlockSpec((d, tn), lambda i, j: (0, j + j0)),
            pl.BlockSpec((1, HEAD_DIM), lambda i, j: (0, 0)),
            pl.BlockSpec((1, HEAD_DIM), lambda i, j: (0, 0)),
        ],
        out_specs=pl.BlockSpec((tm, tn), lambda i, j: (i, j)),
        scratch_shapes=[pltpu.VMEM((tm, d), BF16)],
        compiler_params=_cparams(("parallel", "arbitrary")),
        name="norm_inproj",
    )(x2d, shift, scale, gain.reshape(1, d), w_bf, q_gain.reshape(1, HEAD_DIM), k_gain.reshape(1, HEAD_DIM))


def _natten_kernel(q_ref, k_ref, v_ref, kc_ref, vc_ref, b_ref, o_ref, *, rows):
    rb = pl.program_id(2)
    u = jnp.clip(rb * ATT_ROWS - NA_KH // 2, 0, rows - ATT_KROWS)
    tok0 = pl.multiple_of(u * GRID_W, GRID_W)
    nk = ATT_KROWS * GRID_W
    scale = HEAD_DIM ** -0.5
    for hh in range(ATT_HPG):
        sl = slice(hh * HEAD_DIM, (hh + 1) * HEAD_DIM)
        q = q_ref[0, :, sl]
        kw = k_ref[0, pl.ds(tok0, nk), sl]
        vw = v_ref[0, pl.ds(tok0, nk), sl]
        kc = kc_ref[0, :, sl]
        vc = vc_ref[0, :, sl]
        s_loc = _dot_nt(q, kw) * scale + b_ref[0, hh]
        s_ctx = _dot_nt(q, kc) * scale
        m = jnp.maximum(jnp.max(s_loc, axis=-1, keepdims=True), jnp.max(s_ctx, axis=-1, keepdims=True))
        p_loc = jnp.exp(s_loc - m)
        p_ctx = jnp.exp(s_ctx - m)
        den = jnp.sum(p_loc, axis=-1, keepdims=True) + jnp.sum(p_ctx, axis=-1, keepdims=True)
        o = _dot(p_loc.astype(BF16), vw) + _dot(p_ctx.astype(BF16), vc)
        o_ref[0, :, sl] = (o / den).astype(BF16)


def _natten(p3, pc3, bias, *, na_width, kc_col, vc_col):
    b, n, _ = p3.shape
    lc = pc3.shape[1]
    rows = n // GRID_W
    assert rows % ATT_ROWS == 0 and rows >= ATT_KROWS
    nrb = rows // ATT_ROWS
    cw = ATT_HPG * HEAD_DIM
    ng = na_width // cw
    rq = ATT_ROWS * GRID_W

    def case(rb):
        return jnp.where(rb == 0, 0, jnp.where(rb == nrb - 1, 2, 1))

    return pl.pallas_call(
        functools.partial(_natten_kernel, rows=rows),
        out_shape=jax.ShapeDtypeStruct((b, n, na_width), BF16),
        grid=(b, ng, nrb),
        in_specs=[
            pl.BlockSpec((1, rq, cw), lambda bi, g, rb: (bi, rb, g)),
            pl.BlockSpec((1, n, cw), lambda bi, g, rb: (bi, 0, ng + g)),
            pl.BlockSpec((1, n, cw), lambda bi, g, rb: (bi, 0, 2 * ng + g)),
            pl.BlockSpec((1, lc, cw), lambda bi, g, rb: (bi, 0, kc_col // cw + g)),
            pl.BlockSpec((1, lc, cw), lambda bi, g, rb: (bi, 0, vc_col // cw + g)),
            pl.BlockSpec((1, ATT_HPG, rq, ATT_KROWS * GRID_W), lambda bi, g, rb: (case(rb), g, 0, 0)),
        ],
        out_specs=pl.BlockSpec((1, rq, cw), lambda bi, g, rb: (bi, rb, g)),
        compiler_params=_cparams(("parallel", "parallel", "arbitrary")),
        name="natten",
    )(p3, p3, p3, pc3, pc3, bias)


def _ctx_attn_kernel(q_ref, k_ref, v_ref, o_ref):
    scale = HEAD_DIM ** -0.5
    for hh in range(ATT_HPG):
        sl = slice(hh * HEAD_DIM, (hh + 1) * HEAD_DIM)
        s = _dot_nt(q_ref[0, :, sl], k_ref[0, :, sl]) * scale
        m = jnp.max(s, axis=-1, keepdims=True)
        p = jnp.exp(s - m)
        den = jnp.sum(p, axis=-1, keepdims=True)
        o_ref[0, :, sl] = (_dot(p.astype(BF16), v_ref[0, :, sl]) / den).astype(BF16)


def _ctx_attn(pc3, *, na_width):
    b, lc, _ = pc3.shape
    cw = ATT_HPG * HEAD_DIM
    ng = na_width // cw
    return pl.pallas_call(
        _ctx_attn_kernel,
        out_shape=jax.ShapeDtypeStruct((b, lc, na_width), BF16),
        grid=(b, ng),
        in_specs=[
            pl.BlockSpec((1, lc, cw), lambda bi, g: (bi, 0, g)),
            pl.BlockSpec((1, lc, cw), lambda bi, g: (bi, 0, ng + g)),
            pl.BlockSpec((1, lc, cw), lambda bi, g: (bi, 0, 2 * ng + g)),
        ],
        out_specs=pl.BlockSpec((1, lc, cw), lambda bi, g: (bi, 0, g)),
        compiler_params=_cparams(("parallel", "parallel")),
        name="ctx_attn",
    )(pc3, pc3, pc3)


def _mix_kernel(xin_ref, gb_ref, gc_ref, su_ref, sv_ref, xp_ref, cp_ref, xn_ref, cn_ref,
                cw_ref, sgn_ref, sgw_ref, sgb_ref, o_ref, *, tm, cwid, groups, halo):
    i = pl.program_id(1)
    last = pl.num_programs(1) - 1
    z = gc_ref[0].astype(F32) * xin_ref[0].astype(F32)
    zp = cp_ref[0, halo - 1:halo, :].astype(F32) * xp_ref[0, halo - 1:halo, :].astype(F32)
    zn = cn_ref[0, 0:1, :].astype(F32) * xn_ref[0, 0:1, :].astype(F32)
    zp = jnp.where(i == 0, 0.0, zp)
    zn = jnp.where(i == last, 0.0, zn)
    row = lax.broadcasted_iota(jnp.int32, (tm, cwid), 0)
    z_m1 = jnp.where(row == 0, zp, pltpu.roll(z, 1, axis=0))
    z_p1 = jnp.where(row == tm - 1, zn, pltpu.roll(z, tm - 1, axis=0))
    conv = cw_ref[0:1, :] * z_m1 + cw_ref[1:2, :] * z + cw_ref[2:3, :] * z_p1
    o_ref[0, :, 0:cwid] = (gb_ref[0].astype(F32) * conv).astype(BF16)

    sv = sv_ref[0].astype(F32)
    ms = jnp.mean(sv * sv, axis=-1, keepdims=True)
    vn = (sv * lax.rsqrt(ms + EPS) * sgn_ref[...]).astype(BF16)
    for g in range(groups):
        gs = slice(g * LANES, (g + 1) * LANES)
        for c in range(tm // CHUNK):
            ts = slice(c * CHUNK, (c + 1) * CHUNK)
            mixed = _dot(sgw_ref[g], vn[ts, gs]) + sgb_ref[g]
            o_ref[0, ts, cwid + g * LANES:cwid + (g + 1) * LANES] = (
                su_ref[0, ts, gs].astype(F32) * mixed).astype(BF16)


def _mix(p3, conv_w, sg_norm, sg_w_bf, sg_b, *, col0, cwid, swid):
    b, l, _ = p3.shape
    groups = sg_w_bf.shape[0]
    assert swid == groups * LANES and cwid == swid and l % CHUNK == 0 and col0 % cwid == 0
    tm = _tile(l, 512, CHUNK)
    halo = 16
    nt = l // tm
    c0 = col0 // cwid
    hb = tm // halo
    nhb = l // halo

    def main(k):
        return pl.BlockSpec((1, tm, cwid), lambda bi, i: (bi, i, c0 + k))

    def prev(k):
        return pl.BlockSpec((1, halo, cwid), lambda bi, i: (bi, jnp.maximum(i * hb - 1, 0), c0 + k))

    def nxt(k):
        return pl.BlockSpec((1, halo, cwid), lambda bi, i: (bi, jnp.minimum((i + 1) * hb, nhb - 1), c0 + k))

    full = lambda shape: pl.BlockSpec(shape, lambda bi, i: (0,) * len(shape))
    return pl.pallas_call(
        functools.partial(_mix_kernel, tm=tm, cwid=cwid, groups=groups, halo=halo),
        out_shape=jax.ShapeDtypeStruct((b, l, cwid + swid), BF16),
        grid=(b, nt),
        in_specs=[main(0), main(1), main(2), main(3), main(4), prev(0), prev(2), nxt(0), nxt(2),
                  full((3, cwid)), full((1, swid)), full((groups, CHUNK, CHUNK)), full((groups, CHUNK, 1))],
        out_specs=pl.BlockSpec((1, tm, cwid + swid), lambda bi, i: (bi, i, 0)),
        compiler_params=_cparams(("parallel", "parallel")),
        name="conv_gmlp",
    )(p3, p3, p3, p3, p3, p3, p3, p3, p3, conv_w, sg_norm.reshape(1, swid), sg_w_bf,
      sg_b.reshape(groups, CHUNK, 1))


def _outproj_kernel(a_ref, m_ref, wa_ref, wm_ref, x_ref, g_ref, o_ref):
    acc = _dot(a_ref[...], wa_ref[...]) + _dot(m_ref[...], wm_ref[...])
    o_ref[...] = x_ref[...] + g_ref[0] * acc


def _outproj(a2d, m2d, w_bf, x2d, gate, seq_len):
    t, d = x2d.shape
    ka, km = a2d.shape[1], m2d.shape[1]
    assert ka == km
    tm = _tile(seq_len, 512, 16)
    tn = _tile(d, 1024, LANES)
    nrows = gate.shape[0]
    row = (lambda i: (i * tm) // seq_len) if nrows > 1 else (lambda i: 0)
    return pl.pallas_call(
        _outproj_kernel,
        out_shape=jax.ShapeDtypeStruct((t, d), F32),
        grid=(t // tm, d // tn),
        in_specs=[
            pl.BlockSpec((tm, ka), lambda i, j: (i, 0)),
            pl.BlockSpec((tm, km), lambda i, j: (i, 0)),
            pl.BlockSpec((ka, tn), lambda i, j: (0, j)),
            pl.BlockSpec((km, tn), lambda i, j: (1, j)),
            pl.BlockSpec((tm, tn), lambda i, j: (i, j)),
            pl.BlockSpec((1, 1, tn), lambda i, j: (row(i), 0, j)),
        ],
        out_specs=pl.BlockSpec((tm, tn), lambda i, j: (i, j)),
        compiler_params=_cparams(("parallel", "parallel")),
        name="outproj",
    )(a2d, m2d, w_bf, w_bf, x2d, gate)


def _norm_router_kernel(x_ref, sh_ref, sc_ref, g_ref, wh_ref, wl_ref, h_ref, aff_ref):
    xf = x_ref[...]
    ms = jnp.mean(xf * xf, axis=-1, keepdims=True)
    y = xf * lax.rsqrt(ms + EPS) * g_ref[...]
    h = y * (1.0 + sc_ref[0]) + sh_ref[0]
    h_ref[...] = h
    h_hi = h.astype(BF16)
    h_lo = (h - h_hi.astype(F32)).astype(BF16)
    logits = _dot_nt(wh_ref[...], h_hi) + _dot_nt(wh_ref[...], h_lo) + _dot_nt(wl_ref[...], h_hi)
    m = jnp.max(logits, axis=0, keepdims=True)
    e = jnp.exp(logits - m)
    aff_ref[0] = e / jnp.sum(e, axis=0, keepdims=True)


def _norm_router(x2d, shift, scale, gain, wr_hi, wr_lo, seq_len):
    t, d = x2d.shape
    e = wr_hi.shape[0]
    nseq = t // seq_len
    tm = _tile(seq_len, 512, LANES)
    per = seq_len // tm
    nrows = shift.shape[0]
    row = (lambda i: i // per) if nrows > 1 else (lambda i: 0)
    return pl.pallas_call(
        _norm_router_kernel,
        out_shape=(jax.ShapeDtypeStruct((t, d), F32), jax.ShapeDtypeStruct((nseq, e, seq_len), F32)),
        grid=(t // tm,),
        in_specs=[
            pl.BlockSpec((tm, d), lambda i: (i, 0)),
            pl.BlockSpec((1, 1, d), lambda i: (row(i), 0, 0)),
            pl.BlockSpec((1, 1, d), lambda i: (row(i), 0, 0)),
            pl.BlockSpec((1, d), lambda i: (0, 0)),
            pl.BlockSpec((e, d), lambda i: (0, 0)),
            pl.BlockSpec((e, d), lambda i: (0, 0)),
        ],
        out_specs=(pl.BlockSpec((tm, d), lambda i: (i, 0)),
                   pl.BlockSpec((1, e, tm), lambda i: (i // per, 0, i % per))),
        compiler_params=_cparams(("parallel",)),
        name="norm_router",
    )(x2d, shift, scale, gain.reshape(1, d), wr_hi, wr_lo)


def _topk_kernel(aff_ref, idx_ref, gate_ref, *, nr, cap, capp):
    aff = aff_ref[0, 0]
    bits = pltpu.bitcast(aff, jnp.int32)

    def count(mask):
        c = jnp.sum(mask.astype(F32), axis=0, keepdims=True)
        return jnp.sum(c, axis=1, keepdims=True)

    def search(i, thr):
        cand = thr | jnp.left_shift(jnp.int32(1), 30 - i)
        return jnp.where(count(bits >= cand) >= cap, cand, thr)

    thr = lax.fori_loop(0, 31, search, jnp.zeros((1, 1), jnp.int32))
    gt = bits > thr
    eq = bits == thr
    need = cap - count(gt)

    lane_u = lax.broadcasted_iota(jnp.int32, (LANES, LANES), 0)
    lane_t = lax.broadcasted_iota(jnp.int32, (LANES, LANES), 1)
    tri_incl = (lane_u <= lane_t).astype(BF16)
    row_a = lax.broadcasted_iota(jnp.int32, (nr, nr), 0)
    row_b = lax.broadcasted_iota(jnp.int32, (nr, nr), 1)
    low_strict = (row_b < row_a).astype(BF16)

    def row_offsets(maskf):
        tot = jnp.sum(maskf, axis=1, keepdims=True)
        totb = jnp.broadcast_to(tot, (nr, LANES)).astype(BF16)
        return tot, _dot(low_strict, totb)[:, 0:1]

    eqf = eq.astype(F32)
    _, eq_off = row_offsets(eqf)
    eq_rank = _dot(eq.astype(BF16), tri_incl) - eqf + eq_off
    sel = gt | (eq & (eq_rank < need))
    self32 = sel.astype(F32)
    selb = sel.astype(BF16)

    tot, offx = row_offsets(self32)
    offi = offx + tot
    slot = lax.broadcasted_iota(jnp.int32, (1, capp), 1).astype(F32)
    r_of = jnp.sum((offi <= slot).astype(F32), axis=0, keepdims=True)
    rows_col = lax.broadcasted_iota(jnp.int32, (nr, capp), 0).astype(F32)
    onehot = rows_col == r_of
    onehot_b = onehot.astype(BF16)
    local = slot - jnp.sum(jnp.where(onehot, offx, 0.0), axis=0, keepdims=True)
    incl_t = _dot_nt((lane_t <= lane_u).astype(BF16), selb)
    pref = _dot(incl_t.astype(BF16), onehot_b)
    lane_of = jnp.sum((pref <= local).astype(F32), axis=0, keepdims=True)
    idx = r_of * LANES + lane_of
    idx_ref[0, 0] = jnp.clip(idx, 0, nr * LANES - 1).astype(jnp.int32)

    eye = (lane_u == lane_t).astype(BF16)
    a_hi = aff.astype(BF16)
    r1 = aff - a_hi.astype(F32)
    a_mid = r1.astype(BF16)
    a_lo = (r1 - a_mid.astype(F32)).astype(BF16)
    rows_t = jnp.zeros((LANES, capp), F32)
    for part in (a_hi, a_mid, a_lo):
        part_t = _dot_nt(eye, part).astype(BF16)
        rows_t = rows_t + _dot(part_t, onehot_b)
    lane_col = lax.broadcasted_iota(jnp.int32, (LANES, capp), 0).astype(F32)
    gate_ref[0, 0] = jnp.sum(jnp.where(lane_col == lane_of, rows_t, 0.0), axis=0, keepdims=True)


def _topk(aff, cap):
    s, e, l = aff.shape
    nr = l // LANES
    assert l % (LANES * SUBLANES) == 0
    capp = -(-cap // LANES) * LANES
    shp = (s, e, 1, capp)
    spec = pl.BlockSpec((1, 1, 1, capp), lambda si, ei: (si, ei, 0, 0))
    return pl.pallas_call(
        functools.partial(_topk_kernel, nr=nr, cap=cap, capp=capp),
        out_shape=(jax.ShapeDtypeStruct(shp, jnp.int32), jax.ShapeDtypeStruct(shp, F32)),
        grid=(s, e),
        in_specs=[pl.BlockSpec((1, 1, nr, LANES), lambda si, ei: (si, ei, 0, 0))],
        out_specs=(spec, spec),
        compiler_params=_cparams(("parallel", "parallel")),
        name="expert_topk",
    )(aff.reshape(s, e, nr, LANES))


def _ffn_kernel(idx_ref, h_hbm, gate_ref, wg_ref, wu_ref, wd_ref, o_ref, hbuf, hs, sem,
                *, seq_len, capp, tmr, n_exp):
    e = pl.program_id(0)
    s = pl.program_id(1)
    mt = pl.program_id(2)
    fc = pl.program_id(3)

    @pl.when(fc == 0)
    def _():
        base = (s * n_exp + e) * capp + mt * tmr

        def row_copy(r):
            tok = idx_ref[base + r]
            return pltpu.make_async_copy(h_hbm.at[pl.ds(s * seq_len + tok, 1), :], hbuf.at[pl.ds(r, 1), :], sem)

        def start(r, carry):
            row_copy(r).start()
            return carry

        def wait(r, carry):
            row_copy(r).wait()
            return carry

        lax.fori_loop(0, tmr, start, 0)
        lax.fori_loop(0, tmr, wait, 0)
        hs[...] = hbuf[...].astype(BF16)

    x = hs[...]
    act = (_silu(_dot(x, wg_ref[0])) * _dot(x, wu_ref[0])).astype(BF16)
    y = _dot(act, wd_ref[0])

    @pl.when(fc == 0)
    def _():
        o_ref[0, 0] = y

    @pl.when(fc > 0)
    def _():
        o_ref[0, 0] += y

    @pl.when(fc == pl.num_programs(3) - 1)
    def _():
        o_ref[0, 0] = o_ref[0, 0] * gate_ref[0, 0]


def _expert_ffn(h2d, idx, gate, wg_bf, wu_bf, wd_bf, seq_len, cap):
    d = h2d.shape[1]
    s, e, _, capp = idx.shape
    ff = wg_bf.shape[2]
    tmr = _tile(cap, 512, SUBLANES)
    fch = _tile(ff, 256, LANES)
    gate_col = gate[:, :, 0, :cap].reshape(s, e, cap, 1)
    grid_spec = pltpu.PrefetchScalarGridSpec(
        num_scalar_prefetch=1,
        grid=(e, s, cap // tmr, ff // fch),
        in_specs=[
            pl.BlockSpec(memory_space=pl.ANY),
            pl.BlockSpec((1, 1, tmr, 1), lambda ei, si, mt, fc, idx_r: (si, ei, mt, 0)),
            pl.BlockSpec((1, d, fch), lambda ei, si, mt, fc, idx_r: (ei, 0, fc)),
            pl.BlockSpec((1, d, fch), lambda ei, si, mt, fc, idx_r: (ei, 0, fc)),
            pl.BlockSpec((1, fch, d), lambda ei, si, mt, fc, idx_r: (ei, fc, 0)),
        ],
        out_specs=pl.BlockSpec((1, 1, tmr, d), lambda ei, si, mt, fc, idx_r: (si, ei, mt, 0)),
        scratch_shapes=[pltpu.VMEM((tmr, d), F32), pltpu.VMEM((tmr, d), BF16), pltpu.SemaphoreType.DMA],
    )
    return pl.pallas_call(
        functools.partial(_ffn_kernel, seq_len=seq_len, capp=capp, tmr=tmr, n_exp=e),
        out_shape=jax.ShapeDtypeStruct((s, e, cap, d), F32),
        grid_spec=grid_spec,
        compiler_params=_cparams(("arbitrary", "arbitrary", "arbitrary", "arbitrary")),
        name="expert_ffn",
    )(idx.reshape(-1), h2d, gate_col, wg_bf, wu_bf, wd_bf)


def _combine_kernel(idx_ref, y_ref, g_ref, x_hbm, o_hbm, xbuf, gsem, ssem, *, seq_len, capp, tr, n_seq, n_exp):
    del x_hbm
    e = pl.program_id(0)
    s = pl.program_id(1)
    j = pl.program_id(2)
    nj = pl.num_programs(2)
    k = (e * n_seq + s) * nj + j
    total = n_exp * n_seq * nj
    slot = k % 2

    def rows(kk, sl, sem):
        jj = kk % nj
        ss = (kk // nj) % n_seq
        ee = kk // (nj * n_seq)
        base = (ss * n_exp + ee) * capp + jj * tr

        def one(r):
            tok = idx_ref[base + r]
            return o_hbm.at[pl.ds(ss * seq_len + tok, 1), :], xbuf.at[sl, pl.ds(r, 1), :]

        return one

    def gather_start(kk, sl):
        one = rows(kk, sl, gsem)

        def body(r, c):
            src, dst = one(r)
            pltpu.make_async_copy(src, dst, gsem.at[sl]).start()
            return c

        lax.fori_loop(0, tr, body, 0)

    def gather_wait(kk, sl):
        one = rows(kk, sl, gsem)

        def body(r, c):
            src, dst = one(r)
            pltpu.make_async_copy(src, dst, gsem.at[sl]).wait()
            return c

        lax.fori_loop(0, tr, body, 0)

    def scatter_start(kk, sl):
        one = rows(kk, sl, ssem)

        def body(r, c):
            dst, src = one(r)
            pltpu.make_async_copy(src, dst, ssem.at[sl]).start()
            return c

        lax.fori_loop(0, tr, body, 0)

    def scatter_wait(kk, sl):
        one = rows(kk, sl, ssem)

        def body(r, c):
            dst, src = one(r)
            pltpu.make_async_copy(src, dst, ssem.at[sl]).wait()
            return c

        lax.fori_loop(0, tr, body, 0)

    @pl.when(k == 0)
    def _():
        gather_start(k, slot)

    gather_wait(k, slot)

    @pl.when(k + 1 < total)
    def _():
        @pl.when(k >= 1)
        def _():
            scatter_wait(k - 1, 1 - slot)

        gather_start(k + 1, 1 - slot)

    xbuf[slot] = xbuf[slot] + g_ref[0] * y_ref[0, 0]
    scatter_start(k, slot)

    @pl.when(k == total - 1)
    def _():
        scatter_wait(k, slot)

        @pl.when(total > 1)
        def _():
            scatter_wait(k - 1, 1 - slot)


def _combine(x2d, y, idx, g2, seq_len, cap):
    t, d = x2d.shape
    s, e, _, capp = idx.shape
    tr = _tile(cap, 256, SUBLANES)
    nj = cap // tr
    assert s * nj >= 2, "consecutive steps must touch disjoint rows"
    nrows = g2.shape[0]
    grid_spec = pltpu.PrefetchScalarGridSpec(
        num_scalar_prefetch=1,
        grid=(e, s, nj),
        in_specs=[
            pl.BlockSpec((1, 1, tr, d), lambda ei, si, j, idx_r: (si, ei, j, 0)),
            pl.BlockSpec((1, 1, d), lambda ei, si, j, idx_r: (si if nrows > 1 else 0, 0, 0)),
            pl.BlockSpec(memory_space=pl.ANY),
        ],
        out_specs=pl.BlockSpec(memory_space=pl.ANY),
        scratch_shapes=[pltpu.VMEM((2, tr, d), F32), pltpu.SemaphoreType.DMA((2,)), pltpu.SemaphoreType.DMA((2,))],
    )
    return pl.pallas_call(
        functools.partial(_combine_kernel, seq_len=seq_len, capp=capp, tr=tr, n_seq=s, n_exp=e),
        out_shape=jax.ShapeDtypeStruct((t, d), F32),
        grid_spec=grid_spec,
        input_output_aliases={3: 0},
        compiler_params=_cparams(("arbitrary", "arbitrary", "arbitrary")),
        name="moe_combine",
    )(idx.reshape(-1), y, g2, x2d)


def _moe(x2d, shift, scale, g2, gain, wr_hi, wr_lo, wg_bf, wu_bf, wd_bf, seq_len):
    n_exp = wr_hi.shape[0]
    cap = CAPACITY_FACTOR * seq_len // n_exp
    h2, aff = _norm_router(x2d, shift, scale, gain, wr_hi, wr_lo, seq_len)
    lpad = -(-seq_len // (LANES * SUBLANES)) * (LANES * SUBLANES)
    if lpad != seq_len:
        aff = jnp.pad(aff, ((0, 0), (0, 0), (0, lpad - seq_len)), constant_values=-1.0)
    idx, gate = _topk(aff, cap)
    y = _expert_ffn(h2, idx, gate, wg_bf, wu_bf, wd_bf, seq_len, cap)
    return _combine(x2d, y, idx, g2, seq_len, cap)


def kernel(x, c, ctx, c_ctx, w_ada, b_ada, norm1, norm2, w_in, q_norm, k_norm, rpb, conv_w, sg_norm, sg_w,
           sg_b, w_out, w_router, w_gate, w_up, w_down):
    b, n, d = x.shape
    lc = ctx.shape[1]
    depth = w_ada.shape[0]
    na = d // 2
    cwid = d // 4
    swid = d - na - cwid
    in_cols = w_in.shape[2]
    assert in_cols == 3 * na + 3 * cwid + 2 * swid and b + 1 <= SUBLANES

    cvec = jnp.concatenate([c, c_ctx[None, :], jnp.zeros((SUBLANES - b - 1, d), F32)], axis=0)
    mod = _adaln(cvec, w_ada, b_ada).reshape(depth, SUBLANES, N_MOD, 1, d)

    xl = x.reshape(b * n, d)
    xc = ctx.reshape(b * lc, d)
    for l in range(depth):
        update_ctx = l < depth - 1
        lat = [mod[l, :b, k] for k in range(N_MOD)]
        cx = [mod[l, b:b + 1, k] for k in range(N_MOD)]
        w_in_bf = w_in[l].astype(BF16)
        w_out_bf = w_out[l].astype(BF16)
        sg_w_bf = sg_w[l].astype(BF16)
        wr_t = w_router[l].T
        wr_hi = wr_t.astype(BF16)
        wr_lo = (wr_t - wr_hi.astype(F32)).astype(BF16)
        wg_bf, wu_bf, wd_bf = w_gate[l].astype(BF16), w_up[l].astype(BF16), w_down[l].astype(BF16)
        bias = _bias_table(rpb[l], n // GRID_W)

        c_col0, c_ncols = (0, in_cols) if update_ctx else (na, 2 * na)
        pc = _inproj(xc, cx[0], cx[1], norm1[l], w_in_bf, q_norm[l], k_norm[l], lc,
                     col0=c_col0, ncols=c_ncols, na_width=na).reshape(b, lc, c_ncols)
        p = _inproj(xl, lat[0], lat[1], norm1[l], w_in_bf, q_norm[l], k_norm[l], n,
                    col0=0, ncols=in_cols, na_width=na).reshape(b, n, in_cols)

        a = _natten(p, pc, bias, na_width=na, kc_col=na - c_col0, vc_col=2 * na - c_col0)
        m = _mix(p, conv_w[l], sg_norm[l], sg_w_bf, sg_b[l], col0=3 * na, cwid=cwid, swid=swid)
        xl = _outproj(a.reshape(b * n, na), m.reshape(b * n, cwid + swid), w_out_bf, xl, lat[2], n)
        xl = _moe(xl, lat[3], lat[4], lat[5], norm2[l], wr_hi, wr_lo, wg_bf, wu_bf, wd_bf, n)

        if update_ctx:
            a_c = _ctx_attn(pc, na_width=na)
            m_c = _mix(pc, conv_w[l], sg_norm[l], sg_w_bf, sg_b[l], col0=3 * na, cwid=cwid, swid=swid)
            xc = _outproj(a_c.reshape(b * lc, na), m_c.reshape(b * lc, cwid + swid), w_out_bf, xc, cx[2], lc)
            xc = _moe(xc, cx[3], cx[4], cx[5], norm2[l], wr_hi, wr_lo, wg_bf, wu_bf, wd_bf, lc)
    return xl.reshape(b, n, d)
```

```python
import functools

import jax
import jax.numpy as jnp
from jax import lax
from jax.experimental import pallas as pl
from jax.experimental.pallas import tpu as pltpu

F32 = jnp.float32
BF16 = jnp.bfloat16

GRID_W = 64
HEAD_DIM = 128
NA_KH = 8
NA_KW = 16
CHUNK = 128
N_MOD = 6
CAPACITY_FACTOR = 2
EPS = 1e-6
NEG = -1e30

LANES = 128
SUBLANES = 8
VMEM_LIMIT = 56 * 1024 * 1024

ATT_ROWS = 4
ATT_KROWS = ATT_ROWS + NA_KH
ATT_HPG = 4


def _cparams(sem, vmem=VMEM_LIMIT):
    return pltpu.CompilerParams(dimension_semantics=sem, vmem_limit_bytes=vmem)


def _tile(n, pref, align):
    if n <= pref:
        return n
    t = (pref // align) * align
    while t >= align:
        if n % t == 0:
            return t
        t -= align
    raise ValueError(f"no tile for {n} (pref {pref}, align {align})")


def _dot(a, b):
    return jnp.dot(a, b, preferred_element_type=F32)


def _dot_nt(a, b):
    return lax.dot_general(a, b, (((1,), (1,)), ((), ())), preferred_element_type=F32)


def _silu(v):
    return v * jax.nn.sigmoid(v)


def _adaln_kernel(c_ref, w_ref, b_ref, o_ref):
    c = c_ref[...]
    s = _silu(c).astype(BF16)
    o_ref[0] = _dot(s, w_ref[0].astype(BF16)) + b_ref[0]


def _adaln(cvec, w_ada, b_ada):
    depth, d, cols = w_ada.shape
    rows = cvec.shape[0]
    tn = _tile(cols, 512, LANES)
    return pl.pallas_call(
        _adaln_kernel,
        out_shape=jax.ShapeDtypeStruct((depth, rows, cols), F32),
        grid=(depth, cols // tn),
        in_specs=[
            pl.BlockSpec((rows, d), lambda l, j: (0, 0)),
            pl.BlockSpec((1, d, tn), lambda l, j: (l, 0, j)),
            pl.BlockSpec((1, 1, tn), lambda l, j: (l, 0, j)),
        ],
        out_specs=pl.BlockSpec((1, rows, tn), lambda l, j: (l, 0, j)),
        compiler_params=_cparams(("parallel", "parallel")),
        name="adaln",
    )(cvec, w_ada, b_ada.reshape(depth, 1, cols))


def _att_cases(rows):
    nrb = rows // ATT_ROWS
    cases = []
    for rb in (0, min(1, nrb - 1), nrb - 1):
        r0 = rb * ATT_ROWS
        u = min(max(r0 - NA_KH // 2, 0), rows - ATT_KROWS)
        per_row = []
        for a in range(ATT_ROWS):
            r = r0 + a
            s = min(max(r - NA_KH // 2, 0), rows - NA_KH)
            per_row.append((s - u, s - r + NA_KH - 1))
        cases.append(per_row)
    return cases


def _bias_kernel(rpb_ref, o_ref, *, cases, n_dr, n_dc):
    case = pl.program_id(0)
    h = pl.program_id(1)
    qc = lax.broadcasted_iota(jnp.int32, (GRID_W, GRID_W), 0)
    kc = lax.broadcasted_iota(jnp.int32, (GRID_W, GRID_W), 1)
    cs = jnp.clip(qc - NA_KW // 2, 0, GRID_W - NA_KW)
    col_ok = (kc >= cs) & (kc < cs + NA_KW)
    dc = kc - qc + NA_KW - 1
    neg = jnp.full((GRID_W, GRID_W), NEG, F32)

    def toeplitz(dr):
        t = neg
        for i in range(n_dc):
            t = jnp.where(dc == i, rpb_ref[h, dr, i], t)
        return jnp.where(col_ok, t, neg)

    tiles = [toeplitz(dr) for dr in range(n_dr)]
    for ci, per_row in enumerate(cases):
        @pl.when(case == ci)
        def _(per_row=per_row):
            for a, (j_first, dr_first) in enumerate(per_row):
                for j in range(ATT_KROWS):
                    inside = j_first <= j < j_first + NA_KH
                    t = tiles[dr_first + (j - j_first)] if inside else neg
                    o_ref[0, 0, a * GRID_W:(a + 1) * GRID_W, j * GRID_W:(j + 1) * GRID_W] = t


def _bias_table(rpb_l, rows):
    h, n_dr, n_dc = rpb_l.shape
    cases = _att_cases(rows)
    rq, rk = ATT_ROWS * GRID_W, ATT_KROWS * GRID_W
    return pl.pallas_call(
        functools.partial(_bias_kernel, cases=cases, n_dr=n_dr, n_dc=n_dc),
        out_shape=jax.ShapeDtypeStruct((3, h, rq, rk), F32),
        grid=(3, h),
        in_specs=[pl.BlockSpec(memory_space=pltpu.SMEM)],
        out_specs=pl.BlockSpec((1, 1, rq, rk), lambda c, hh: (c, hh, 0, 0)),
        compiler_params=_cparams(("parallel", "parallel")),
        name="rpb_bias",
    )(rpb_l)


def _inproj_kernel(x_ref, sh_ref, sc_ref, g_ref, w_ref, qg_ref, kg_ref, o_ref, h_scr, *, j0, nq, tn):
    j = pl.program_id(1)

    @pl.when(j == 0)
    def _():
        xf = x_ref[...]
        ms = jnp.mean(xf * xf, axis=-1, keepdims=True)
        y = xf * lax.rsqrt(ms + EPS) * g_ref[...]
        h_scr[...] = (y * (1.0 + sc_ref[0]) + sh_ref[0]).astype(BF16)

    acc = _dot(h_scr[...], w_ref[...])
    jj = j + j0

    def head_norm(gain_ref):
        for hh in range(tn // HEAD_DIM):
            sl = slice(hh * HEAD_DIM, (hh + 1) * HEAD_DIM)
            blk = acc[:, sl]
            ms = jnp.mean(blk * blk, axis=-1, keepdims=True)
            o_ref[:, sl] = (blk * lax.rsqrt(ms + EPS) * gain_ref[...]).astype(BF16)

    @pl.when(jj < nq)
    def _():
        head_norm(qg_ref)

    @pl.when((jj >= nq) & (jj < 2 * nq))
    def _():
        head_norm(kg_ref)

    @pl.when(jj >= 2 * nq)
    def _():
        o_ref[...] = acc.astype(BF16)


def _inproj(x2d, shift, scale, gain, w_bf, q_gain, k_gain, seq_len, *, col0, ncols, na_width):
    t, d = x2d.shape
    tm = _tile(seq_len, 512, 16)
    tn = _tile(na_width // 2, 1024, HEAD_DIM)
    assert col0 % tn == 0 and ncols % tn == 0 and t % tm == 0
    nrows = shift.shape[0]
    row = (lambda i: (i * tm) // seq_len) if nrows > 1 else (lambda i: 0)
    j0 = col0 // tn
    return pl.pallas_call(
        functools.partial(_inproj_kernel, j0=j0, nq=na_width // tn, tn=tn),
        out_shape=jax.ShapeDtypeStruct((t, ncols), BF16),
        grid=(t // tm, ncols // tn),
        in_specs=[
            pl.BlockSpec((tm, d), lambda i, j: (i, 0)),
            pl.BlockSpec((1, 1, d), lambda i, j: (row(i), 0, 0)),
            pl.BlockSpec((1, 1, d), lambda i, j: (row(i), 0, 0)),
            pl.BlockSpec((1, d), lambda i, j: (0, 0)),
            pl.BlockSpec((d, tn), lambda i, j: (0, j + j0)),
            pl.BlockSpec((1, HEAD_DIM), lambda i, j: (0, 0)),
            pl.BlockSpec((1, HEAD_DIM), lambda i, j: (0, 0)),
        ],
        out_specs=pl.BlockSpec((tm, tn), lambda i, j: (i, j)),
        scratch_shapes=[pltpu.VMEM((tm, d), BF16)],
        compiler_params=_cparams(("parallel", "arbitrary")),
        name="norm_inproj",
    )(x2d, shift, scale, gain.reshape(1, d), w_bf, q_gain.reshape(1, HEAD_DIM), k_gain.reshape(1, HEAD_DIM))


def _natten_kernel(q_ref, k_ref, v_ref, kc_ref, vc_ref, b_ref, o_ref, *, rows):
    rb = pl.program_id(2)
    u = jnp.clip(rb * ATT_ROWS - NA_KH // 2, 0, rows - ATT_KROWS)
    tok0 = pl.multiple_of(u * GRID_W, GRID_W)
    nk = ATT_KROWS * GRID_W
    scale = HEAD_DIM ** -0.5
    for hh in range(ATT_HPG):
        sl = slice(hh * HEAD_DIM, (hh + 1) * HEAD_DIM)
        q = q_ref[0, :, sl]
        kw = k_ref[0, pl.ds(tok0, nk), sl]
        vw = v_ref[0, pl.ds(tok0, nk), sl]
        kc = kc_ref[0, :, sl]
        vc = vc_ref[0, :, sl]
        s_loc = _dot_nt(q, kw) * scale + b_ref[0, hh]
        s_ctx = _dot_nt(q, kc) * scale
        m = jnp.maximum(jnp.max(s_loc, axis=-1, keepdims=True), jnp.max(s_ctx, axis=-1, keepdims=True))
        p_loc = jnp.exp(s_loc - m)
        p_ctx = jnp.exp(s_ctx - m)
        den = jnp.sum(p_loc, axis=-1, keepdims=True) + jnp.sum(p_ctx, axis=-1, keepdims=True)
        o = _dot(p_loc.astype(BF16), vw) + _dot(p_ctx.astype(BF16), vc)
        o_ref[0, :, sl] = (o / den).astype(BF16)


def _natten(p3, pc3, bias, *, na_width, kc_col, vc_col):
    b, n, _ = p3.shape
    lc = pc3.shape[1]
    rows = n // GRID_W
    assert rows % ATT_ROWS == 0 and rows >= ATT_KROWS
    nrb = rows // ATT_ROWS
    cw = ATT_HPG * HEAD_DIM
    ng = na_width // cw
    rq = ATT_ROWS * GRID_W

    def case(rb):
        return jnp.where(rb == 0, 0, jnp.where(rb == nrb - 1, 2, 1))

    return pl.pallas_call(
        functools.partial(_natten_kernel, rows=rows),
        out_shape=jax.ShapeDtypeStruct((b, n, na_width), BF16),
        grid=(b, ng, nrb),
        in_specs=[
            pl.BlockSpec((1, rq, cw), lambda bi, g, rb: (bi, rb, g)),
            pl.BlockSpec((1, n, cw), lambda bi, g, rb: (bi, 0, ng + g)),
            pl.BlockSpec((1, n, cw), lambda bi, g, rb: (bi, 0, 2 * ng + g)),
            pl.BlockSpec((1, lc, cw), lambda bi, g, rb: (bi, 0, kc_col // cw + g)),
            pl.BlockSpec((1, lc, cw), lambda bi, g, rb: (bi, 0, vc_col // cw + g)),
            pl.BlockSpec((1, ATT_HPG, rq, ATT_KROWS * GRID_W), lambda bi, g, rb: (case(rb), g, 0, 0)),
        ],
        out_specs=pl.BlockSpec((1, rq, cw), lambda bi, g, rb: (bi, rb, g)),
        compiler_params=_cparams(("parallel", "parallel", "arbitrary")),
        name="natten",
    )(p3, p3, p3, pc3, pc3, bias)


def _ctx_attn_kernel(q_ref, k_ref, v_ref, o_ref):
    scale = HEAD_DIM ** -0.5
    for hh in range(ATT_HPG):
        sl = slice(hh * HEAD_DIM, (hh + 1) * HEAD_DIM)
        s = _dot_nt(q_ref[0, :, sl], k_ref[0, :, sl]) * scale
        m = jnp.max(s, axis=-1, keepdims=True)
        p = jnp.exp(s - m)
        den = jnp.sum(p, axis=-1, keepdims=True)
        o_ref[0, :, sl] = (_dot(p.astype(BF16), v_ref[0, :, sl]) / den).astype(BF16)


def _ctx_attn(pc3, *, na_width):
    b, lc, _ = pc3.shape
    cw = ATT_HPG * HEAD_DIM
    ng = na_width // cw
    return pl.pallas_call(
        _ctx_attn_kernel,
        out_shape=jax.ShapeDtypeStruct((b, lc, na_width), BF16),
        grid=(b, ng),
        in_specs=[
            pl.BlockSpec((1, lc, cw), lambda bi, g: (bi, 0, g)),
            pl.BlockSpec((1, lc, cw), lambda bi, g: (bi, 0, ng + g)),
            pl.BlockSpec((1, lc, cw), lambda bi, g: (bi, 0, 2 * ng + g)),
        ],
        out_specs=pl.BlockSpec((1, lc, cw), lambda bi, g: (bi, 0, g)),
        compiler_params=_cparams(("parallel", "parallel")),
        name="ctx_attn",
    )(pc3, pc3, pc3)


def _mix_kernel(xin_ref, gb_ref, gc_ref, su_ref, sv_ref, xp_ref, cp_ref, xn_ref, cn_ref,
                cw_ref, sgn_ref, sgw_ref, sgb_ref, o_ref, *, tm, cwid, groups, halo):
    i = pl.program_id(1)
    last = pl.num_programs(1) - 1
    z = gc_ref[0].astype(F32) * xin_ref[0].astype(F32)
    zp = cp_ref[0, halo - 1:halo, :].astype(F32) * xp_ref[0, halo - 1:halo, :].astype(F32)
    zn = cn_ref[0, 0:1, :].astype(F32) * xn_ref[0, 0:1, :].astype(F32)
    zp = jnp.where(i == 0, 0.0, zp)
    zn = jnp.where(i == last, 0.0, zn)
    row = lax.broadcasted_iota(jnp.int32, (tm, cwid), 0)
    z_m1 = jnp.where(row == 0, zp, pltpu.roll(z, 1, axis=0))
    z_p1 = jnp.where(row == tm - 1, zn, pltpu.roll(z, tm - 1, axis=0))
    conv = cw_ref[0:1, :] * z_m1 + cw_ref[1:2, :] * z + cw_ref[2:3, :] * z_p1
    o_ref[0, :, 0:cwid] = (gb_ref[0].astype(F32) * conv).astype(BF16)

    sv = sv_ref[0].astype(F32)
    ms = jnp.mean(sv * sv, axis=-1, keepdims=True)
    vn = (sv * lax.rsqrt(ms + EPS) * sgn_ref[...]).astype(BF16)
    for g in range(groups):
        gs = slice(g * LANES, (g + 1) * LANES)
        for c in range(tm // CHUNK):
            ts = slice(c * CHUNK, (c + 1) * CHUNK)
            mixed = _dot(sgw_ref[g], vn[ts, gs]) + sgb_ref[g]
            o_ref[0, ts, cwid + g * LANES:cwid + (g + 1) * LANES] = (
                su_ref[0, ts, gs].astype(F32) * mixed).astype(BF16)


def _mix(p3, conv_w, sg_norm, sg_w_bf, sg_b, *, col0, cwid, swid):
    b, l, _ = p3.shape
    groups = sg_w_bf.shape[0]
    assert swid == groups * LANES and cwid == swid and l % CHUNK == 0 and col0 % cwid == 0
    tm = _tile(l, 512, CHUNK)
    halo = 16
    nt = l // tm
    c0 = col0 // cwid
    hb = tm // halo
    nhb = l // halo

    def main(k):
        return pl.BlockSpec((1, tm, cwid), lambda bi, i: (bi, i, c0 + k))

    def prev(k):
        return pl.BlockSpec((1, halo, cwid), lambda bi, i: (bi, jnp.maximum(i * hb - 1, 0), c0 + k))

    def nxt(k):
        return pl.BlockSpec((1, halo, cwid), lambda bi, i: (bi, jnp.minimum((i + 1) * hb, nhb - 1), c0 + k))

    full = lambda shape: pl.BlockSpec(shape, lambda bi, i: (0,) * len(shape))
    return pl.pallas_call(
        functools.partial(_mix_kernel, tm=tm, cwid=cwid, groups=groups, halo=halo),
        out_shape=jax.ShapeDtypeStruct((b, l, cwid + swid), BF16),
        grid=(b, nt),
        in_specs=[main(0), main(1), main(2), main(3), main(4), prev(0), prev(2), nxt(0), nxt(2),
                  full((3, cwid)), full((1, swid)), full((groups, CHUNK, CHUNK)), full((groups, CHUNK, 1))],
        out_specs=pl.BlockSpec((1, tm, cwid + swid), lambda bi, i: (bi, i, 0)),
        compiler_params=_cparams(("parallel", "parallel")),
        name="conv_gmlp",
    )(p3, p3, p3, p3, p3, p3, p3, p3, p3, conv_w, sg_norm.reshape(1, swid), sg_w_bf,
      sg_b.reshape(groups, CHUNK, 1))


def _outproj_kernel(a_ref, m_ref, wa_ref, wm_ref, x_ref, g_ref, o_ref):
    acc = _dot(a_ref[...], wa_ref[...]) + _dot(m_ref[...], wm_ref[...])
    o_ref[...] = x_ref[...] + g_ref[0] * acc


def _outproj(a2d, m2d, w_bf, x2d, gate, seq_len):
    t, d = x2d.shape
    ka, km = a2d.shape[1], m2d.shape[1]
    assert ka == km
    tm = _tile(seq_len, 1024, 16)
    tn = _tile(d, 1024, LANES)
    nrows = gate.shape[0]
    row = (lambda i: (i * tm) // seq_len) if nrows > 1 else (lambda i: 0)
    return pl.pallas_call(
        _outproj_kernel,
        out_shape=jax.ShapeDtypeStruct((t, d), F32),
        grid=(t // tm, d // tn),
        in_specs=[
            pl.BlockSpec((tm, ka), lambda i, j: (i, 0)),
            pl.BlockSpec((tm, km), lambda i, j: (i, 0)),
            pl.BlockSpec((ka, tn), lambda i, j: (0, j)),
            pl.BlockSpec((km, tn), lambda i, j: (1, j)),
            pl.BlockSpec((tm, tn), lambda i, j: (i, j)),
            pl.BlockSpec((1, 1, tn), lambda i, j: (row(i), 0, j)),
        ],
        out_specs=pl.BlockSpec((tm, tn), lambda i, j: (i, j)),
        compiler_params=_cparams(("parallel", "parallel")),
        name="outproj",
    )(a2d, m2d, w_bf, w_bf, x2d, gate)


def _norm_router_kernel(x_ref, sh_ref, sc_ref, g_ref, wh_ref, wl_ref, h_ref, aff_ref):
    xf = x_ref[...]
    ms = jnp.mean(xf * xf, axis=-1, keepdims=True)
    y = xf * lax.rsqrt(ms + EPS) * g_ref[...]
    h = y * (1.0 + sc_ref[0]) + sh_ref[0]
    h_hi = h.astype(BF16)
    half = h.shape[1] // 2
    lo = pltpu.bitcast(h_hi[:, :half].astype(F32), jnp.uint32) >> 16
    hi = pltpu.bitcast(h_hi[:, half:].astype(F32), jnp.uint32) & jnp.uint32(0xFFFF0000)
    h_ref[...] = hi | lo
    h_lo = (h - h_hi.astype(F32)).astype(BF16)
    logits = _dot_nt(wh_ref[...], h_hi) + _dot_nt(wh_ref[...], h_lo) + _dot_nt(wl_ref[...], h_hi)
    m = jnp.max(logits, axis=0, keepdims=True)
    e = jnp.exp(logits - m)
    aff_ref[0] = e / jnp.sum(e, axis=0, keepdims=True)


def _norm_router(x2d, shift, scale, gain, wr_hi, wr_lo, seq_len):
    t, d = x2d.shape
    e = wr_hi.shape[0]
    nseq = t // seq_len
    tm = _tile(seq_len, 512, LANES)
    per = seq_len // tm
    nrows = shift.shape[0]
    row = (lambda i: i // per) if nrows > 1 else (lambda i: 0)
    return pl.pallas_call(
        _norm_router_kernel,
        out_shape=(jax.ShapeDtypeStruct((t, d // 2), jnp.uint32), jax.ShapeDtypeStruct((nseq, e, seq_len), F32)),
        grid=(t // tm,),
        in_specs=[
            pl.BlockSpec((tm, d), lambda i: (i, 0)),
            pl.BlockSpec((1, 1, d), lambda i: (row(i), 0, 0)),
            pl.BlockSpec((1, 1, d), lambda i: (row(i), 0, 0)),
            pl.BlockSpec((1, d), lambda i: (0, 0)),
            pl.BlockSpec((e, d), lambda i: (0, 0)),
            pl.BlockSpec((e, d), lambda i: (0, 0)),
        ],
        out_specs=(pl.BlockSpec((tm, d // 2), lambda i: (i, 0)),
                   pl.BlockSpec((1, e, tm), lambda i: (i // per, 0, i % per))),
        compiler_params=_cparams(("parallel",)),
        name="norm_router",
    )(x2d, shift, scale, gain.reshape(1, d), wr_hi, wr_lo)


def _topk_kernel(aff_ref, idx_ref, gate_ref, *, nr, cap, capp):
    aff = aff_ref[0, 0]
    bits = pltpu.bitcast(aff, jnp.int32)

    def count(mask):
        c = jnp.sum(mask.astype(F32), axis=0, keepdims=True)
        return jnp.sum(c, axis=1, keepdims=True)

    def search(i, thr):
        cand = thr | jnp.left_shift(jnp.int32(1), 30 - i)
        return jnp.where(count(bits >= cand) >= cap, cand, thr)

    thr = lax.fori_loop(0, 31, search, jnp.zeros((1, 1), jnp.int32))
    gt = bits > thr
    eq = bits == thr
    need = cap - count(gt)

    lane_u = lax.broadcasted_iota(jnp.int32, (LANES, LANES), 0)
    lane_t = lax.broadcasted_iota(jnp.int32, (LANES, LANES), 1)
    tri_incl = (lane_u <= lane_t).astype(BF16)
    row_a = lax.broadcasted_iota(jnp.int32, (nr, nr), 0)
    row_b = lax.broadcasted_iota(jnp.int32, (nr, nr), 1)
    low_strict = (row_b < row_a).astype(BF16)

    def row_offsets(maskf):
        tot = jnp.sum(maskf, axis=1, keepdims=True)
        totb = jnp.broadcast_to(tot, (nr, LANES)).astype(BF16)
        return tot, _dot(low_strict, totb)[:, 0:1]

    eqf = eq.astype(F32)
    _, eq_off = row_offsets(eqf)
    eq_rank = _dot(eq.astype(BF16), tri_incl) - eqf + eq_off
    sel = gt | (eq & (eq_rank < need))
    self32 = sel.astype(F32)
    selb = sel.astype(BF16)

    tot, offx = row_offsets(self32)
    offi = offx + tot
    slot = lax.broadcasted_iota(jnp.int32, (1, capp), 1).astype(F32)
    r_of = jnp.sum((offi <= slot).astype(F32), axis=0, keepdims=True)
    rows_col = lax.broadcasted_iota(jnp.int32, (nr, capp), 0).astype(F32)
    onehot = rows_col == r_of
    onehot_b = onehot.astype(BF16)
    local = slot - jnp.sum(jnp.where(onehot, offx, 0.0), axis=0, keepdims=True)
    incl_t = _dot_nt((lane_t <= lane_u).astype(BF16), selb)
    pref = _dot(incl_t.astype(BF16), onehot_b)
    lane_of = jnp.sum((pref <= local).astype(F32), axis=0, keepdims=True)
    idx = r_of * LANES + lane_of
    idx_ref[0, 0] = jnp.clip(idx, 0, nr * LANES - 1).astype(jnp.int32)

    eye = (lane_u == lane_t).astype(BF16)
    a_hi = aff.astype(BF16)
    r1 = aff - a_hi.astype(F32)
    a_mid = r1.astype(BF16)
    a_lo = (r1 - a_mid.astype(F32)).astype(BF16)
    rows_t = jnp.zeros((LANES, capp), F32)
    for part in (a_hi, a_mid, a_lo):
        part_t = _dot_nt(eye, part).astype(BF16)
        rows_t = rows_t + _dot(part_t, onehot_b)
    lane_col = lax.broadcasted_iota(jnp.int32, (LANES, capp), 0).astype(F32)
    gate_ref[0, 0] = jnp.sum(jnp.where(lane_col == lane_of, rows_t, 0.0), axis=0, keepdims=True)


def _topk(aff, cap):
    s, e, l = aff.shape
    nr = l // LANES
    assert l % (LANES * SUBLANES) == 0
    capp = -(-cap // LANES) * LANES
    shp = (s, e, 1, capp)
    spec = pl.BlockSpec((1, 1, 1, capp), lambda si, ei: (si, ei, 0, 0))
    return pl.pallas_call(
        functools.partial(_topk_kernel, nr=nr, cap=cap, capp=capp),
        out_shape=(jax.ShapeDtypeStruct(shp, jnp.int32), jax.ShapeDtypeStruct(shp, F32)),
        grid=(s, e),
        in_specs=[pl.BlockSpec((1, 1, nr, LANES), lambda si, ei: (si, ei, 0, 0))],
        out_specs=(spec, spec),
        compiler_params=_cparams(("parallel", "parallel")),
        name="expert_topk",
    )(aff.reshape(s, e, nr, LANES))


def _ffn_kernel(*refs, n_src, segs, n_exp, m_rows, rb, n_up, n_dn, fu, half):
    idx_refs = refs[:n_src]
    h_hbms = refs[n_src:2 * n_src]
    gate_ref, wg_ref, wu_ref, wd_ref, o_ref, hsp, act, sem = refs[2 * n_src:]
    e = pl.program_id(0)
    st = pl.program_id(1)
    hi_mask = jnp.uint32(0xFFFF0000)

    def issue_rows(ee, lo, hi):
        for row0, nrows, si, s, seq_len, capp in segs:
            a, b = max(lo, row0), min(hi, row0 + nrows)
            if a >= b:
                continue

            def body(r, c, row0=row0, si=si, s=s, seq_len=seq_len, capp=capp):
                tok = idx_refs[si][(s * n_exp + ee) * capp + (r - row0)]
                pltpu.make_async_copy(h_hbms[si].at[pl.ds(s * seq_len + tok, 1), :],
                                      hsp.at[pl.ds(r, 1), :], sem).start()
                return c

            lax.fori_loop(a, b, body, 0, unroll=8)

    @pl.when((e == 0) & (st == 0))
    def _():
        issue_rows(0, 0, m_rows)

    @pl.when(st == 0)
    def _():
        pltpu.make_async_copy(h_hbms[0].at[pl.ds(0, m_rows), :], hsp, sem).wait()

    @pl.when(st < n_up)
    def _():
        w = jnp.concatenate([wg_ref[0].astype(BF16), wu_ref[0].astype(BF16)], axis=1)
        for rbi in range(m_rows // rb):
            rows = slice(rbi * rb, (rbi + 1) * rb)
            hp = hsp[rows, :]
            lo = pltpu.bitcast(hp << 16, F32).astype(BF16)
            hi = pltpu.bitcast(hp & hi_mask, F32).astype(BF16)
            gu = _dot(lo, w[:half]) + _dot(hi, w[half:])
            a = (_silu(gu[:, :fu]) * gu[:, fu:]).astype(BF16)
            for k in range(n_up):
                @pl.when(st == k)
                def _(k=k, a=a, rows=rows):
                    act[rows, k * fu:(k + 1) * fu] = a

    @pl.when(st >= n_up)
    def _():
        for k in range(n_dn):
            @pl.when((st == n_up + k) & (e + 1 < n_exp))
            def _(k=k):
                issue_rows(e + 1, (m_rows * k) // n_dn, (m_rows * (k + 1)) // n_dn)

        wd = wd_ref[0].astype(BF16)
        for rbi in range(m_rows // rb):
            rows = slice(rbi * rb, (rbi + 1) * rb)
            y = _dot(act[rows, :], wd) * gate_ref[0, rows, :]
            o_ref[0, rows, :] = y.astype(BF16)


def _expert_ffn(routed, w_gate, w_up, w_down):
    n_exp, d, ff = w_gate.shape
    n_src = len(routed)
    segs, gates, row0 = [], [], 0
    for si, (hp, idx, gate, seq_len, cap) in enumerate(routed):
        n_seq, _, _, capp = idx.shape
        for s in range(n_seq):
            segs.append((row0, cap, si, s, seq_len, capp))
            row0 += cap
        gates.append(jnp.transpose(gate[:, :, 0, :cap], (1, 0, 2)).reshape(n_exp, n_seq * cap))
    m_rows = row0
    gate_all = jnp.concatenate(gates, axis=1).reshape(n_exp, m_rows, 1)
    rb = _tile(m_rows, -(-m_rows // 4), 16) if m_rows % 64 == 0 else m_rows
    fu = _tile(ff, 2 * LANES, LANES)
    dnw = _tile(d, 512, LANES)
    n_up, n_dn = ff // fu, d // dnw
    assert routed[0][0].shape[0] >= m_rows

    def up_idx(ei, st, *_):
        return ei, 0, jnp.minimum(st, n_up - 1)

    def dn_idx(ei, st, *_):
        return ei, 0, jnp.clip(st - n_up, 0, n_dn - 1)

    grid_spec = pltpu.PrefetchScalarGridSpec(
        num_scalar_prefetch=n_src,
        grid=(n_exp, n_up + n_dn),
        in_specs=[pl.BlockSpec(memory_space=pl.ANY)] * n_src + [
            pl.BlockSpec((1, m_rows, 1), lambda ei, st, *_: (ei, 0, 0)),
            pl.BlockSpec((1, d, fu), up_idx),
            pl.BlockSpec((1, d, fu), up_idx),
            pl.BlockSpec((1, ff, dnw), dn_idx),
        ],
        out_specs=pl.BlockSpec((1, m_rows, dnw), dn_idx),
        scratch_shapes=[pltpu.VMEM((m_rows, d // 2), jnp.uint32), pltpu.VMEM((m_rows, ff), BF16),
                        pltpu.SemaphoreType.DMA],
    )
    return pl.pallas_call(
        functools.partial(_ffn_kernel, n_src=n_src, segs=tuple(segs), n_exp=n_exp, m_rows=m_rows, rb=rb,
                          n_up=n_up, n_dn=n_dn, fu=fu, half=d // 2),
        out_shape=jax.ShapeDtypeStruct((n_exp, m_rows, d), BF16),
        grid_spec=grid_spec,
        compiler_params=_cparams(("arbitrary", "arbitrary")),
        name="expert_ffn",
    )(*[r[1].reshape(-1) for r in routed], *[r[0] for r in routed], gate_all, w_gate, w_up, w_down)


def _combine_kernel(idx_ref, y_ref, g_ref, x_hbm, o_hbm, xbuf, gsem, ssem, *, seq_len, capp, tr, n_seq, n_exp):
    del x_hbm
    e = pl.program_id(0)
    s = pl.program_id(1)
    j = pl.program_id(2)
    nj = pl.num_programs(2)
    k = (e * n_seq + s) * nj + j
    total = n_exp * n_seq * nj
    slot = k % 2

    def row_pairs(kk, sl):
        jj = kk % nj
        ss = (kk // nj) % n_seq
        ee = kk // (nj * n_seq)
        base = (ss * n_exp + ee) * capp + jj * tr

        def one(r):
            tok = idx_ref[base + r]
            return o_hbm.at[pl.ds(ss * seq_len + tok, 1), :], xbuf.at[sl, pl.ds(r, 1), :]

        return one

    def gather_start(kk, sl):
        one = row_pairs(kk, sl)

        def body(r, c):
            src, dst = one(r)
            pltpu.make_async_copy(src, dst, gsem.at[sl]).start()
            return c

        lax.fori_loop(0, tr, body, 0, unroll=8)

    def scatter_start(kk, sl):
        one = row_pairs(kk, sl)

        def body(r, c):
            dst, src = one(r)
            pltpu.make_async_copy(src, dst, ssem.at[sl]).start()
            return c

        lax.fori_loop(0, tr, body, 0, unroll=8)

    def gather_wait(sl):
        pltpu.make_async_copy(o_hbm.at[pl.ds(0, tr), :], xbuf.at[sl], gsem.at[sl]).wait()

    def scatter_wait(sl):
        pltpu.make_async_copy(xbuf.at[sl], o_hbm.at[pl.ds(0, tr), :], ssem.at[sl]).wait()

    @pl.when(k == 0)
    def _():
        gather_start(k, slot)

    gather_wait(slot)

    @pl.when(k + 1 < total)
    def _():
        @pl.when(k >= 1)
        def _():
            scatter_wait(1 - slot)

        gather_start(k + 1, 1 - slot)

    xbuf[slot] = xbuf[slot] + g_ref[0] * y_ref[0].astype(F32)
    scatter_start(k, slot)

    @pl.when(k == total - 1)
    def _():
        scatter_wait(slot)

        @pl.when(total > 1)
        def _():
            scatter_wait(1 - slot)


def _combine(x2d, y, idx, g2, seq_len, cap, row0):
    t, d = x2d.shape
    s, e, _, capp = idx.shape
    tr = _tile(cap, 256, 16)
    nj = cap // tr
    assert s * nj >= 2, "consecutive steps must touch disjoint rows"
    assert row0 % tr == 0 and t >= tr
    nrows = g2.shape[0]
    grid_spec = pltpu.PrefetchScalarGridSpec(
        num_scalar_prefetch=1,
        grid=(e, s, nj),
        in_specs=[
            pl.BlockSpec((1, tr, d), lambda ei, si, j, idx_r: (ei, row0 // tr + si * nj + j, 0)),
            pl.BlockSpec((1, 1, d), lambda ei, si, j, idx_r: (si if nrows > 1 else 0, 0, 0)),
            pl.BlockSpec(memory_space=pl.ANY),
        ],
        out_specs=pl.BlockSpec(memory_space=pl.ANY),
        scratch_shapes=[pltpu.VMEM((2, tr, d), F32), pltpu.SemaphoreType.DMA((2,)), pltpu.SemaphoreType.DMA((2,))],
    )
    return pl.pallas_call(
        functools.partial(_combine_kernel, seq_len=seq_len, capp=capp, tr=tr, n_seq=s, n_exp=e),
        out_shape=jax.ShapeDtypeStruct((t, d), F32),
        grid_spec=grid_spec,
        input_output_aliases={3: 0},
        compiler_params=_cparams(("arbitrary", "arbitrary", "arbitrary")),
        name="moe_combine",
    )(idx.reshape(-1), y, g2, x2d)


def _moe(srcs, gain, wr_hi, wr_lo, w_gate, w_up, w_down):
    n_exp = wr_hi.shape[0]
    routed = []
    for x2d, shift, scale, _, seq_len in srcs:
        cap = CAPACITY_FACTOR * seq_len // n_exp
        hp, aff = _norm_router(x2d, shift, scale, gain, wr_hi, wr_lo, seq_len)
        lpad = -(-seq_len // (LANES * SUBLANES)) * (LANES * SUBLANES)
        if lpad != seq_len:
            aff = jnp.pad(aff, ((0, 0), (0, 0), (0, lpad - seq_len)), constant_values=-1.0)
        idx, gate = _topk(aff, cap)
        routed.append((hp, idx, gate, seq_len, cap))
    y = _expert_ffn(routed, w_gate, w_up, w_down)
    outs, row0 = [], 0
    for (x2d, _, _, g2, seq_len), (_, idx, _, _, cap) in zip(srcs, routed):
        outs.append(_combine(x2d, y, idx, g2, seq_len, cap, row0))
        row0 += idx.shape[0] * cap
    return outs


def kernel(x, c, ctx, c_ctx, w_ada, b_ada, norm1, norm2, w_in, q_norm, k_norm, rpb, conv_w, sg_norm, sg_w,
           sg_b, w_out, w_router, w_gate, w_up, w_down):
    b, n, d = x.shape
    lc = ctx.shape[1]
    depth = w_ada.shape[0]
    na = d // 2
    cwid = d // 4
    swid = d - na - cwid
    in_cols = w_in.shape[2]
    assert in_cols == 3 * na + 3 * cwid + 2 * swid and b + 1 <= SUBLANES

    cvec = jnp.concatenate([c, c_ctx[None, :], jnp.zeros((SUBLANES - b - 1, d), F32)], axis=0)
    mod = _adaln(cvec, w_ada, b_ada).reshape(depth, SUBLANES, N_MOD, 1, d)

    xl = x.reshape(b * n, d)
    xc = ctx.reshape(b * lc, d)
    for l in range(depth):
        update_ctx = l < depth - 1
        lat = [mod[l, :b, k] for k in range(N_MOD)]
        cx = [mod[l, b:b + 1, k] for k in range(N_MOD)]
        w_in_bf = w_in[l].astype(BF16)
        w_out_bf = w_out[l].astype(BF16)
        sg_w_bf = sg_w[l].astype(BF16)
        wr_t = w_router[l].T
        wr_hi = wr_t.astype(BF16)
        wr_lo = (wr_t - wr_hi.astype(F32)).astype(BF16)
        bias = _bias_table(rpb[l], n // GRID_W)

        c_col0, c_ncols = (0, in_cols) if update_ctx else (na, 2 * na)
        pc = _inproj(xc, cx[0], cx[1], norm1[l], w_in_bf, q_norm[l], k_norm[l], lc,
                     col0=c_col0, ncols=c_ncols, na_width=na).reshape(b, lc, c_ncols)
        p = _inproj(xl, lat[0], lat[1], norm1[l], w_in_bf, q_norm[l], k_norm[l], n,
                    col0=0, ncols=in_cols, na_width=na).reshape(b, n, in_cols)

        a = _natten(p, pc, bias, na_width=na, kc_col=na - c_col0, vc_col=2 * na - c_col0)
        m = _mix(p, conv_w[l], sg_norm[l], sg_w_bf, sg_b[l], col0=3 * na, cwid=cwid, swid=swid)
        xl = _outproj(a.reshape(b * n, na), m.reshape(b * n, cwid + swid), w_out_bf, xl, lat[2], n)
        srcs = [(xl, lat[3], lat[4], lat[5], n)]
        if update_ctx:
            a_c = _ctx_attn(pc, na_width=na)
            m_c = _mix(pc, conv_w[l], sg_norm[l], sg_w_bf, sg_b[l], col0=3 * na, cwid=cwid, swid=swid)
            xc = _outproj(a_c.reshape(b * lc, na), m_c.reshape(b * lc, cwid + swid), w_out_bf, xc, cx[2], lc)
            srcs.append((xc, cx[3], cx[4], cx[5], lc))
        outs = _moe(srcs, norm2[l], wr_hi, wr_lo, w_gate[l], w_up[l], w_down[l])
        xl = outs[0]
        if update_ctx:
            xc = outs[1]
    return xl.reshape(b, n, d)
```

```python
import functools

import jax
import jax.numpy as jnp
from jax import lax
from jax.experimental import pallas as pl
from jax.experimental.pallas import tpu as pltpu

F32 = jnp.float32
BF16 = jnp.bfloat16

GRID_W = 64
HEAD_DIM = 128
NA_KH = 8
NA_KW = 16
CHUNK = 128
N_MOD = 6
CAPACITY_FACTOR = 2
EPS = 1e-6
NEG = -1e30

LANES = 128
SUBLANES = 8
VMEM_LIMIT = 56 * 1024 * 1024

ATT_ROWS = 4
ATT_KROWS = ATT_ROWS + NA_KH
ATT_HPG = 4
NORM_ROWS = 64
LOG2E = 1.4426950408889634


def _cparams(sem, vmem=VMEM_LIMIT):
    return pltpu.CompilerParams(dimension_semantics=sem, vmem_limit_bytes=vmem)


def _tile(n, pref, align):
    if n <= pref:
        return n
    t = (pref // align) * align
    while t >= align:
        if n % t == 0:
            return t
        t -= align
    raise ValueError(f"no tile for {n} (pref {pref}, align {align})")


def _dot(a, b):
    return jnp.dot(a, b, preferred_element_type=F32)


def _dot_nt(a, b):
    return lax.dot_general(a, b, (((1,), (1,)), ((), ())), preferred_element_type=F32)


def _silu(v):
    return v * jax.nn.sigmoid(v)


def _adaln_kernel(c_ref, w_ref, b_ref, o_ref):
    c = c_ref[...]
    s = _silu(c).astype(BF16)
    o_ref[0] = _dot(s, w_ref[0].astype(BF16)) + b_ref[0]


def _adaln(cvec, w_ada, b_ada):
    depth, d, cols = w_ada.shape
    rows = cvec.shape[0]
    tn = _tile(cols, 512, LANES)
    return pl.pallas_call(
        _adaln_kernel,
        out_shape=jax.ShapeDtypeStruct((depth, rows, cols), F32),
        grid=(depth, cols // tn),
        in_specs=[
            pl.BlockSpec((rows, d), lambda l, j: (0, 0)),
            pl.BlockSpec((1, d, tn), lambda l, j: (l, 0, j)),
            pl.BlockSpec((1, 1, tn), lambda l, j: (l, 0, j)),
        ],
        out_specs=pl.BlockSpec((1, rows, tn), lambda l, j: (l, 0, j)),
        compiler_params=_cparams(("parallel", "parallel")),
        name="adaln",
    )(cvec, w_ada, b_ada.reshape(depth, 1, cols))


def _att_cases(rows):
    nrb = rows // ATT_ROWS
    cases = []
    for rb in (0, min(1, nrb - 1), nrb - 1):
        r0 = rb * ATT_ROWS
        u = min(max(r0 - NA_KH // 2, 0), rows - ATT_KROWS)
        per_row = []
        for a in range(ATT_ROWS):
            r = r0 + a
            s = min(max(r - NA_KH // 2, 0), rows - NA_KH)
            per_row.append((s - u, s - r + NA_KH - 1))
        cases.append(per_row)
    return cases


def _bias_kernel(rpb_ref, o_ref, *, cases, n_dr, n_dc):
    case = pl.program_id(0)
    h = pl.program_id(1)
    qc = lax.broadcasted_iota(jnp.int32, (GRID_W, GRID_W), 0)
    kc = lax.broadcasted_iota(jnp.int32, (GRID_W, GRID_W), 1)
    cs = jnp.clip(qc - NA_KW // 2, 0, GRID_W - NA_KW)
    col_ok = (kc >= cs) & (kc < cs + NA_KW)
    dc = kc - qc + NA_KW - 1
    neg = jnp.full((GRID_W, GRID_W), NEG, F32)

    def toeplitz(dr):
        t = neg
        for i in range(n_dc):
            t = jnp.where(dc == i, rpb_ref[h, dr, i] * LOG2E, t)
        return jnp.where(col_ok, t, neg)

    tiles = [toeplitz(dr) for dr in range(n_dr)]
    for ci, per_row in enumerate(cases):
        @pl.when(case == ci)
        def _(per_row=per_row):
            for a, (j_first, dr_first) in enumerate(per_row):
                for j in range(ATT_KROWS):
                    inside = j_first <= j < j_first + NA_KH
                    t = tiles[dr_first + (j - j_first)] if inside else neg
                    o_ref[0, 0, a * GRID_W:(a + 1) * GRID_W, j * GRID_W:(j + 1) * GRID_W] = t


def _bias_table(rpb_l, rows):
    h, n_dr, n_dc = rpb_l.shape
    cases = _att_cases(rows)
    rq, rk = ATT_ROWS * GRID_W, ATT_KROWS * GRID_W
    return pl.pallas_call(
        functools.partial(_bias_kernel, cases=cases, n_dr=n_dr, n_dc=n_dc),
        out_shape=jax.ShapeDtypeStruct((3, h, rq, rk), F32),
        grid=(3, h),
        in_specs=[pl.BlockSpec(memory_space=pltpu.SMEM)],
        out_specs=pl.BlockSpec((1, 1, rq, rk), lambda c, hh: (c, hh, 0, 0)),
        compiler_params=_cparams(("parallel", "parallel")),
        name="rpb_bias",
    )(rpb_l)


def _inproj_kernel(x_hbm, sh_ref, sc_ref, g_ref, w_ref, qg_ref, kg_ref, o_ref, xbuf, h_scr, sem, *, j0, nq, tn, tm):
    i = pl.program_id(0)
    j = pl.program_id(1)

    def x_copy(tile):
        return pltpu.make_async_copy(x_hbm.at[pl.ds(tile * tm, tm), :], xbuf, sem)

    @pl.when((i == 0) & (j == 0))
    def _():
        x_copy(0).start()

    @pl.when(j == 0)
    def _():
        x_copy(i).wait()
        rc = NORM_ROWS if tm % NORM_ROWS == 0 else tm

        def chunk(c, carry):
            rows = pl.ds(pl.multiple_of(c * rc, rc), rc)
            xf = xbuf[rows, :]
            ms = jnp.mean(xf * xf, axis=-1, keepdims=True)
            y = xf * lax.rsqrt(ms + EPS) * g_ref[...]
            h_scr[rows, :] = (y * (1.0 + sc_ref[0]) + sh_ref[0]).astype(BF16)
            return carry

        lax.fori_loop(0, tm // rc, chunk, 0)

    @pl.when((j == 1) & (i + 1 < pl.num_programs(0)))
    def _():
        x_copy(i + 1).start()

    acc = _dot(h_scr[...], w_ref[...])
    jj = j + j0

    def head_norm(gain):
        for hh in range(tn // HEAD_DIM):
            sl = slice(hh * HEAD_DIM, (hh + 1) * HEAD_DIM)
            blk = acc[:, sl]
            ms = jnp.mean(blk * blk, axis=-1, keepdims=True)
            o_ref[:, sl] = (blk * lax.rsqrt(ms + EPS) * gain).astype(BF16)

    @pl.when(jj < nq)
    def _():
        head_norm(qg_ref[...] * (HEAD_DIM ** -0.5 * LOG2E))

    @pl.when((jj >= nq) & (jj < 2 * nq))
    def _():
        head_norm(kg_ref[...])

    @pl.when(jj >= 2 * nq)
    def _():
        o_ref[...] = acc.astype(BF16)


def _inproj(x2d, shift, scale, gain, w_bf, q_gain, k_gain, seq_len, *, col0, ncols, na_width):
    t, d = x2d.shape
    tm = _tile(seq_len, 1024, 16)
    tn = _tile(na_width // 2, 1024, HEAD_DIM)
    assert col0 % tn == 0 and ncols % tn == 0 and t % tm == 0
    assert ncols // tn >= 2, "the next token tile is prefetched at column step 1"
    nrows = shift.shape[0]
    row = (lambda i: (i * tm) // seq_len) if nrows > 1 else (lambda i: 0)
    j0 = col0 // tn
    return pl.pallas_call(
        functools.partial(_inproj_kernel, j0=j0, nq=na_width // tn, tn=tn, tm=tm),
        out_shape=jax.ShapeDtypeStruct((t, ncols), BF16),
        grid=(t // tm, ncols // tn),
        in_specs=[
            pl.BlockSpec(memory_space=pl.ANY),
            pl.BlockSpec((1, 1, d), lambda i, j: (row(i), 0, 0)),
            pl.BlockSpec((1, 1, d), lambda i, j: (row(i), 0, 0)),
            pl.BlockSpec((1, d), lambda i, j: (0, 0)),
            pl.BlockSpec((d, tn), lambda i, j: (0, j + j0)),
            pl.BlockSpec((1, HEAD_DIM), lambda i, j: (0, 0)),
            pl.BlockSpec((1, HEAD_DIM), lambda i, j: (0, 0)),
        ],
        out_specs=pl.BlockSpec((tm, tn), lambda i, j: (i, j)),
        scratch_shapes=[pltpu.VMEM((tm, d), F32), pltpu.VMEM((tm, d), BF16), pltpu.SemaphoreType.DMA],
        compiler_params=_cparams(("arbitrary", "arbitrary")),
        name="norm_inproj",
    )(x2d, shift, scale, gain.reshape(1, d), w_bf, q_gain.reshape(1, HEAD_DIM), k_gain.reshape(1, HEAD_DIM))


def _natten_kernel(q_ref, k_ref, v_ref, kc_ref, vc_ref, b_ref, o_ref, *, rows):
    rb = pl.program_id(2)
    u = jnp.clip(rb * ATT_ROWS - NA_KH // 2, 0, rows - ATT_KROWS)
    tok0 = pl.multiple_of(u * GRID_W, GRID_W)
    nk = ATT_KROWS * GRID_W
    for hh in range(ATT_HPG):
        sl = slice(hh * HEAD_DIM, (hh + 1) * HEAD_DIM)
        q = q_ref[0, :, sl]
        kw = k_ref[0, pl.ds(tok0, nk), sl]
        vw = v_ref[0, pl.ds(tok0, nk), sl]
        kc = kc_ref[0, :, sl]
        vc = vc_ref[0, :, sl]
        s_loc = _dot_nt(q, kw) + b_ref[0, hh]
        s_ctx = _dot_nt(q, kc)
        m = jnp.maximum(jnp.max(s_loc, axis=-1, keepdims=True), jnp.max(s_ctx, axis=-1, keepdims=True))
        p_loc = jnp.exp2(s_loc - m)
        p_ctx = jnp.exp2(s_ctx - m)
        den = jnp.sum(p_loc, axis=-1, keepdims=True) + jnp.sum(p_ctx, axis=-1, keepdims=True)
        o = _dot(p_loc.astype(BF16), vw) + _dot(p_ctx.astype(BF16), vc)
        o_ref[0, :, sl] = (o / den).astype(BF16)


def _natten(p3, pc3, bias, *, na_width, kc_col, vc_col):
    b, n, _ = p3.shape
    lc = pc3.shape[1]
    rows = n // GRID_W
    assert rows % ATT_ROWS == 0 and rows >= ATT_KROWS
    nrb = rows // ATT_ROWS
    cw = ATT_HPG * HEAD_DIM
    ng = na_width // cw
    rq = ATT_ROWS * GRID_W

    def case(rb):
        return jnp.where(rb == 0, 0, jnp.where(rb == nrb - 1, 2, 1))

    return pl.pallas_call(
        functools.partial(_natten_kernel, rows=rows),
        out_shape=jax.ShapeDtypeStruct((b, n, na_width), BF16),
        grid=(b, ng, nrb),
        in_specs=[
            pl.BlockSpec((1, rq, cw), lambda bi, g, rb: (bi, rb, g)),
            pl.BlockSpec((1, n, cw), lambda bi, g, rb: (bi, 0, ng + g)),
            pl.BlockSpec((1, n, cw), lambda bi, g, rb: (bi, 0, 2 * ng + g)),
            pl.BlockSpec((1, lc, cw), lambda bi, g, rb: (bi, 0, kc_col // cw + g)),
            pl.BlockSpec((1, lc, cw), lambda bi, g, rb: (bi, 0, vc_col // cw + g)),
            pl.BlockSpec((1, ATT_HPG, rq, ATT_KROWS * GRID_W), lambda bi, g, rb: (case(rb), g, 0, 0)),
        ],
        out_specs=pl.BlockSpec((1, rq, cw), lambda bi, g, rb: (bi, rb, g)),
        compiler_params=_cparams(("parallel", "parallel", "arbitrary")),
        name="natten",
    )(p3, p3, p3, pc3, pc3, bias)


def _ctx_attn_kernel(q_ref, k_ref, v_ref, o_ref):
    for hh in range(ATT_HPG):
        sl = slice(hh * HEAD_DIM, (hh + 1) * HEAD_DIM)
        s = _dot_nt(q_ref[0, :, sl], k_ref[0, :, sl])
        m = jnp.max(s, axis=-1, keepdims=True)
        p = jnp.exp2(s - m)
        den = jnp.sum(p, axis=-1, keepdims=True)
        o_ref[0, :, sl] = (_dot(p.astype(BF16), v_ref[0, :, sl]) / den).astype(BF16)


def _ctx_attn(pc3, *, na_width):
    b, lc, _ = pc3.shape
    cw = ATT_HPG * HEAD_DIM
    ng = na_width // cw
    return pl.pallas_call(
        _ctx_attn_kernel,
        out_shape=jax.ShapeDtypeStruct((b, lc, na_width), BF16),
        grid=(b, ng),
        in_specs=[
            pl.BlockSpec((1, lc, cw), lambda bi, g: (bi, 0, g)),
            pl.BlockSpec((1, lc, cw), lambda bi, g: (bi, 0, ng + g)),
            pl.BlockSpec((1, lc, cw), lambda bi, g: (bi, 0, 2 * ng + g)),
        ],
        out_specs=pl.BlockSpec((1, lc, cw), lambda bi, g: (bi, 0, g)),
        compiler_params=_cparams(("parallel", "parallel")),
        name="ctx_attn",
    )(pc3, pc3, pc3)


def _mix_kernel(xin_ref, gb_ref, gc_ref, su_ref, sv_ref, xp_ref, cp_ref, xn_ref, cn_ref,
                cw_ref, sgn_ref, sgw_ref, sgb_ref, o_ref, *, tm, cwid, groups, halo):
    i = pl.program_id(1)
    last = pl.num_programs(1) - 1
    z = gc_ref[0].astype(F32) * xin_ref[0].astype(F32)
    zp = cp_ref[0, halo - 1:halo, :].astype(F32) * xp_ref[0, halo - 1:halo, :].astype(F32)
    zn = cn_ref[0, 0:1, :].astype(F32) * xn_ref[0, 0:1, :].astype(F32)
    zp = jnp.where(i == 0, 0.0, zp)
    zn = jnp.where(i == last, 0.0, zn)
    row = lax.broadcasted_iota(jnp.int32, (tm, cwid), 0)
    z_m1 = jnp.where(row == 0, zp, pltpu.roll(z, 1, axis=0))
    z_p1 = jnp.where(row == tm - 1, zn, pltpu.roll(z, tm - 1, axis=0))
    conv = cw_ref[0:1, :] * z_m1 + cw_ref[1:2, :] * z + cw_ref[2:3, :] * z_p1
    o_ref[0, :, 0:cwid] = (gb_ref[0].astype(F32) * conv).astype(BF16)

    sv = sv_ref[0].astype(F32)
    ms = jnp.mean(sv * sv, axis=-1, keepdims=True)
    vn = (sv * lax.rsqrt(ms + EPS) * sgn_ref[...]).astype(BF16)
    for g in range(groups):
        gs = slice(g * LANES, (g + 1) * LANES)
        for c in range(tm // CHUNK):
            ts = slice(c * CHUNK, (c + 1) * CHUNK)
            mixed = _dot(sgw_ref[g], vn[ts, gs]) + sgb_ref[g]
            o_ref[0, ts, cwid + g * LANES:cwid + (g + 1) * LANES] = (
                su_ref[0, ts, gs].astype(F32) * mixed).astype(BF16)


def _mix(p3, conv_w, sg_norm, sg_w_bf, sg_b, *, col0, cwid, swid):
    b, l, _ = p3.shape
    groups = sg_w_bf.shape[0]
    assert swid == groups * LANES and cwid == swid and l % CHUNK == 0 and col0 % cwid == 0
    tm = _tile(l, 512, CHUNK)
    halo = 16
    nt = l // tm
    c0 = col0 // cwid
    hb = tm // halo
    nhb = l // halo

    def main(k):
        return pl.BlockSpec((1, tm, cwid), lambda bi, i: (bi, i, c0 + k))

    def prev(k):
        return pl.BlockSpec((1, halo, cwid), lambda bi, i: (bi, jnp.maximum(i * hb - 1, 0), c0 + k))

    def nxt(k):
        return pl.BlockSpec((1, halo, cwid), lambda bi, i: (bi, jnp.minimum((i + 1) * hb, nhb - 1), c0 + k))

    full = lambda shape: pl.BlockSpec(shape, lambda bi, i: (0,) * len(shape))
    return pl.pallas_call(
        functools.partial(_mix_kernel, tm=tm, cwid=cwid, groups=groups, halo=halo),
        out_shape=jax.ShapeDtypeStruct((b, l, cwid + swid), BF16),
        grid=(b, nt),
        in_specs=[main(0), main(1), main(2), main(3), main(4), prev(0), prev(2), nxt(0), nxt(2),
                  full((3, cwid)), full((1, swid)), full((groups, CHUNK, CHUNK)), full((groups, CHUNK, 1))],
        out_specs=pl.BlockSpec((1, tm, cwid + swid), lambda bi, i: (bi, i, 0)),
        compiler_params=_cparams(("parallel", "parallel")),
        name="conv_gmlp",
    )(p3, p3, p3, p3, p3, p3, p3, p3, p3, conv_w, sg_norm.reshape(1, swid), sg_w_bf,
      sg_b.reshape(groups, CHUNK, 1))


def _outproj_kernel(a_ref, m_ref, wa_ref, wm_ref, x_ref, g_ref, o_ref):
    acc = _dot(a_ref[...], wa_ref[...]) + _dot(m_ref[...], wm_ref[...])
    o_ref[...] = x_ref[...] + g_ref[0] * acc


def _outproj(a2d, m2d, w_bf, x2d, gate, seq_len):
    t, d = x2d.shape
    ka, km = a2d.shape[1], m2d.shape[1]
    assert ka == km
    tm = _tile(seq_len, 1024, 16)
    tn = _tile(d, 1024, LANES)
    nrows = gate.shape[0]
    row = (lambda i: (i * tm) // seq_len) if nrows > 1 else (lambda i: 0)
    return pl.pallas_call(
        _outproj_kernel,
        out_shape=jax.ShapeDtypeStruct((t, d), F32),
        grid=(t // tm, d // tn),
        in_specs=[
            pl.BlockSpec((tm, ka), lambda i, j: (i, 0)),
            pl.BlockSpec((tm, km), lambda i, j: (i, 0)),
            pl.BlockSpec((ka, tn), lambda i, j: (0, j)),
            pl.BlockSpec((km, tn), lambda i, j: (1, j)),
            pl.BlockSpec((tm, tn), lambda i, j: (i, j)),
            pl.BlockSpec((1, 1, tn), lambda i, j: (row(i), 0, j)),
        ],
        out_specs=pl.BlockSpec((tm, tn), lambda i, j: (i, j)),
        compiler_params=_cparams(("parallel", "parallel")),
        name="outproj",
    )(a2d, m2d, w_bf, w_bf, x2d, gate)


def _norm_router_kernel(x_ref, sh_ref, sc_ref, g_ref, wh_ref, wl_ref, h_ref, aff_ref):
    xf = x_ref[...]
    ms = jnp.mean(xf * xf, axis=-1, keepdims=True)
    y = xf * lax.rsqrt(ms + EPS) * g_ref[...]
    h = y * (1.0 + sc_ref[0]) + sh_ref[0]
    h_hi = h.astype(BF16)
    half = h.shape[1] // 2
    lo = pltpu.bitcast(h_hi[:, :half].astype(F32), jnp.uint32) >> 16
    hi = pltpu.bitcast(h_hi[:, half:].astype(F32), jnp.uint32) & jnp.uint32(0xFFFF0000)
    h_ref[...] = hi | lo
    h_lo = (h - h_hi.astype(F32)).astype(BF16)
    logits = _dot_nt(wh_ref[...], h_hi) + _dot_nt(wh_ref[...], h_lo) + _dot_nt(wl_ref[...], h_hi)
    m = jnp.max(logits, axis=0, keepdims=True)
    e = jnp.exp(logits - m)
    aff_ref[0] = e / jnp.sum(e, axis=0, keepdims=True)


def _norm_router(x2d, shift, scale, gain, wr_hi, wr_lo, seq_len):
    t, d = x2d.shape
    e = wr_hi.shape[0]
    nseq = t // seq_len
    tm = _tile(seq_len, 512, LANES)
    per = seq_len // tm
    nrows = shift.shape[0]
    row = (lambda i: i // per) if nrows > 1 else (lambda i: 0)
    return pl.pallas_call(
        _norm_router_kernel,
        out_shape=(jax.ShapeDtypeStruct((t, d // 2), jnp.uint32), jax.ShapeDtypeStruct((nseq, e, seq_len), F32)),
        grid=(t // tm,),
        in_specs=[
            pl.BlockSpec((tm, d), lambda i: (i, 0)),
            pl.BlockSpec((1, 1, d), lambda i: (row(i), 0, 0)),
            pl.BlockSpec((1, 1, d), lambda i: (row(i), 0, 0)),
            pl.BlockSpec((1, d), lambda i: (0, 0)),
            pl.BlockSpec((e, d), lambda i: (0, 0)),
            pl.BlockSpec((e, d), lambda i: (0, 0)),
        ],
        out_specs=(pl.BlockSpec((tm, d // 2), lambda i: (i, 0)),
                   pl.BlockSpec((1, e, tm), lambda i: (i // per, 0, i % per))),
        compiler_params=_cparams(("parallel",)),
        name="norm_router",
    )(x2d, shift, scale, gain.reshape(1, d), wr_hi, wr_lo)


def _topk_kernel(aff_ref, idx_ref, gate_ref, thr_scr, *, nr, cap, capp, n_exp):
    bits_all = pltpu.bitcast(aff_ref[0], jnp.int32)

    def count_all(mask):
        c = jnp.sum(mask.astype(F32), axis=1, keepdims=True)
        return jnp.sum(c, axis=2, keepdims=True)

    def search(i, thr):
        cand = thr | jnp.left_shift(jnp.int32(1), 30 - i)
        return jnp.where(count_all(bits_all >= cand) >= cap, cand, thr)

    thr_scr[...] = lax.fori_loop(0, 31, search, jnp.zeros((n_exp, 1, 1), jnp.int32))

    def per_expert(e, carry):
        _topk_compact(aff_ref[0, e], thr_scr[e], idx_ref.at[0, e], gate_ref.at[0, e], nr=nr, cap=cap, capp=capp)
        return carry

    lax.fori_loop(0, n_exp, per_expert, 0)


def _topk_compact(aff, thr, idx_ref, gate_ref, *, nr, cap, capp):
    bits = pltpu.bitcast(aff, jnp.int32)

    def count(mask):
        c = jnp.sum(mask.astype(F32), axis=0, keepdims=True)
        return jnp.sum(c, axis=1, keepdims=True)

    gt = bits > thr
    eq = bits == thr
    need = cap - count(gt)

    lane_u = lax.broadcasted_iota(jnp.int32, (LANES, LANES), 0)
    lane_t = lax.broadcasted_iota(jnp.int32, (LANES, LANES), 1)
    tri_incl = (lane_u <= lane_t).astype(BF16)
    row_a = lax.broadcasted_iota(jnp.int32, (nr, nr), 0)
    row_b = lax.broadcasted_iota(jnp.int32, (nr, nr), 1)
    low_strict = (row_b < row_a).astype(BF16)

    def row_offsets(maskf):
        tot = jnp.sum(maskf, axis=1, keepdims=True)
        totb = jnp.broadcast_to(tot, (nr, LANES)).astype(BF16)
        return tot, _dot(low_strict, totb)[:, 0:1]

    eqf = eq.astype(F32)
    _, eq_off = row_offsets(eqf)
    eq_rank = _dot(eq.astype(BF16), tri_incl) - eqf + eq_off
    sel = gt | (eq & (eq_rank < need))
    self32 = sel.astype(F32)
    selb = sel.astype(BF16)

    tot, offx = row_offsets(self32)
    offi = offx + tot
    slot = lax.broadcasted_iota(jnp.int32, (1, capp), 1).astype(F32)
    r_of = jnp.sum((offi <= slot).astype(F32), axis=0, keepdims=True)
    rows_col = lax.broadcasted_iota(jnp.int32, (nr, capp), 0).astype(F32)
    onehot = rows_col == r_of
    onehot_b = onehot.astype(BF16)
    local = slot - jnp.sum(jnp.where(onehot, offx, 0.0), axis=0, keepdims=True)
    incl_t = _dot_nt((lane_t <= lane_u).astype(BF16), selb)
    pref = _dot(incl_t.astype(BF16), onehot_b)
    lane_of = jnp.sum((pref <= local).astype(F32), axis=0, keepdims=True)
    idx = r_of * LANES + lane_of
    idx_ref[...] = jnp.clip(idx, 0, nr * LANES - 1).astype(jnp.int32)

    eye = (lane_u == lane_t).astype(BF16)
    a_hi = aff.astype(BF16)
    r1 = aff - a_hi.astype(F32)
    a_mid = r1.astype(BF16)
    a_lo = (r1 - a_mid.astype(F32)).astype(BF16)
    rows_t = jnp.zeros((LANES, capp), F32)
    for part in (a_hi, a_mid, a_lo):
        part_t = _dot_nt(eye, part).astype(BF16)
        rows_t = rows_t + _dot(part_t, onehot_b)
    lane_col = lax.broadcasted_iota(jnp.int32, (LANES, capp), 0).astype(F32)
    gate_ref[...] = jnp.sum(jnp.where(lane_col == lane_of, rows_t, 0.0), axis=0, keepdims=True)


def _topk(aff, cap):
    s, e, l = aff.shape
    nr = l // LANES
    assert l % (LANES * SUBLANES) == 0
    capp = -(-cap // LANES) * LANES
    shp = (s, e, 1, capp)
    spec = pl.BlockSpec((1, e, 1, capp), lambda si: (si, 0, 0, 0))
    return pl.pallas_call(
        functools.partial(_topk_kernel, nr=nr, cap=cap, capp=capp, n_exp=e),
        out_shape=(jax.ShapeDtypeStruct(shp, jnp.int32), jax.ShapeDtypeStruct(shp, F32)),
        grid=(s,),
        in_specs=[pl.BlockSpec((1, e, nr, LANES), lambda si: (si, 0, 0, 0))],
        out_specs=(spec, spec),
        scratch_shapes=[pltpu.VMEM((e, 1, 1), jnp.int32)],
        compiler_params=_cparams(("parallel",)),
        name="expert_topk",
    )(aff.reshape(s, e, nr, LANES))


def _ffn_kernel(*refs, n_src, segs, n_exp, m_rows, lat_rows, per_step, rb, n_up, n_dn, fu, half):
    idx_refs = refs[:n_src]
    h_hbms = refs[n_src:2 * n_src]
    gate_ref, wg_ref, wu_ref, wd_ref, o_ref, hsp, act, sem = refs[2 * n_src:]
    e = pl.program_id(0)
    st = pl.program_id(1)
    hi_mask = jnp.uint32(0xFFFF0000)

    def issue_rows(ee, lo, hi):
        for row0, nrows, si, s, seq_len, capp in segs:
            a, b = max(lo, row0), min(hi, row0 + nrows)
            if a >= b:
                continue

            def body(r, c, row0=row0, si=si, s=s, seq_len=seq_len, capp=capp):
                tok = idx_refs[si][(s * n_exp + ee) * capp + (r - row0)]
                pltpu.make_async_copy(h_hbms[si].at[pl.ds(s * seq_len + tok, 1), :],
                                      hsp.at[pl.ds(r, 1), :], sem).start()
                return c

            lax.fori_loop(a, b, body, 0, unroll=8)

    def wait_rows():
        pltpu.make_async_copy(h_hbms[0].at[pl.ds(0, m_rows), :], hsp, sem).wait()

    @pl.when((e == 0) & (st == 0))
    def _():
        issue_rows(0, 0, m_rows)

    @pl.when(st == 0)
    def _():
        wait_rows()

    @pl.when(st < n_up)
    def _():
        w = jnp.concatenate([wg_ref[0, 0].astype(BF16), wu_ref[0, 0].astype(BF16)], axis=1)
        for rbi in range(m_rows // rb):
            rows = slice(rbi * rb, (rbi + 1) * rb)
            hp = hsp[rows, :]
            lo = pltpu.bitcast(hp << 16, F32).astype(BF16)
            hi = pltpu.bitcast(hp & hi_mask, F32).astype(BF16)
            gu = _dot(lo, w[:half]) + _dot(hi, w[half:])
            a = (_silu(gu[:, :fu]) * gu[:, fu:]).astype(BF16)
            for k in range(n_up):
                @pl.when(st == k)
                def _(k=k, a=a, rows=rows):
                    act[rows, k * fu:(k + 1) * fu] = a

    @pl.when(st >= n_up)
    def _():
        k = st - n_up
        ee = jnp.minimum(e + 1, n_exp - 1)
        _, cap0, _, _, seq_len0, capp0 = segs[0]
        per_seq = cap0 // per_step
        s0 = k // per_seq
        base_idx = (s0 * n_exp + ee) * capp0 + (k % per_seq) * per_step
        base_src = s0 * seq_len0
        base_dst = k * per_step

        @pl.when(k == 0)
        def _():
            issue_rows(ee, lat_rows, m_rows)

        wd = wd_ref[0, 0].astype(BF16)
        nblk = m_rows // rb
        for rbi in range(nblk):
            for r in range((per_step * rbi) // nblk, (per_step * (rbi + 1)) // nblk):
                tok = idx_refs[0][base_idx + r]
                pltpu.make_async_copy(h_hbms[0].at[pl.ds(base_src + tok, 1), :],
                                      hsp.at[pl.ds(base_dst + r, 1), :], sem).start()
            rows = slice(rbi * rb, (rbi + 1) * rb)
            y = _dot(act[rows, :], wd) * gate_ref[0, rows, :]
            o_ref[0, rows, :] = y.astype(BF16)

        @pl.when((e == n_exp - 1) & (k == n_dn - 1))
        def _():
            wait_rows()


def _expert_ffn(routed, w_gate, w_up, w_down, layer):
    _, n_exp, d, ff = w_gate.shape
    n_src = len(routed)
    segs, gates, row0 = [], [], 0
    for si, (hp, idx, gate, seq_len, cap) in enumerate(routed):
        n_seq, _, _, capp = idx.shape
        for s in range(n_seq):
            segs.append((row0, cap, si, s, seq_len, capp))
            row0 += cap
        gates.append(jnp.transpose(gate[:, :, 0, :cap], (1, 0, 2)).reshape(n_exp, n_seq * cap))
    m_rows = row0
    gate_all = jnp.concatenate(gates, axis=1).reshape(n_exp, m_rows, 1)
    rb = _tile(m_rows, -(-m_rows // 4), 16) if m_rows % 64 == 0 else m_rows
    fu = _tile(ff, 2 * LANES, LANES)
    dnw = _tile(d, 512, LANES)
    n_up, n_dn = ff // fu, d // dnw
    lat_rows = routed[0][1].shape[0] * routed[0][4]
    per_step = lat_rows // n_dn
    assert routed[0][0].shape[0] >= m_rows
    assert lat_rows % n_dn == 0 and routed[0][4] % per_step == 0

    def up_idx(ei, st, *_):
        return layer, ei, 0, jnp.minimum(st, n_up - 1)

    def dn_idx(ei, st, *_):
        return ei, 0, jnp.clip(st - n_up, 0, n_dn - 1)

    grid_spec = pltpu.PrefetchScalarGridSpec(
        num_scalar_prefetch=n_src,
        grid=(n_exp, n_up + n_dn),
        in_specs=[pl.BlockSpec(memory_space=pl.ANY)] * n_src + [
            pl.BlockSpec((1, m_rows, 1), lambda ei, st, *_: (ei, 0, 0)),
            pl.BlockSpec((1, 1, d, fu), up_idx),
            pl.BlockSpec((1, 1, d, fu), up_idx),
            pl.BlockSpec((1, 1, ff, dnw), lambda ei, st, *_: (layer,) + dn_idx(ei, st)),
        ],
        out_specs=pl.BlockSpec((1, m_rows, dnw), dn_idx),
        scratch_shapes=[pltpu.VMEM((m_rows, d // 2), jnp.uint32), pltpu.VMEM((m_rows, ff), BF16),
                        pltpu.SemaphoreType.DMA],
    )
    return pl.pallas_call(
        functools.partial(_ffn_kernel, n_src=n_src, segs=tuple(segs), n_exp=n_exp, m_rows=m_rows,
                          lat_rows=lat_rows, per_step=per_step, rb=rb, n_up=n_up, n_dn=n_dn, fu=fu, half=d // 2),
        out_shape=jax.ShapeDtypeStruct((n_exp, m_rows, d), BF16),
        grid_spec=grid_spec,
        compiler_params=_cparams(("arbitrary", "arbitrary")),
        name="expert_ffn",
    )(*[r[1].reshape(-1) for r in routed], *[r[0] for r in routed], gate_all, w_gate, w_up, w_down)


def _combine_kernel(idx_ref, y_ref, g_ref, x_hbm, o_hbm, xbuf, gsem, ssem, *, seq_len, capp, tr, n_seq, n_exp):
    del x_hbm
    e = pl.program_id(0)
    s = pl.program_id(1)
    j = pl.program_id(2)
    nj = pl.num_programs(2)
    k = (e * n_seq + s) * nj + j
    total = n_exp * n_seq * nj
    slot = k % 2

    def row_pairs(kk, sl):
        jj = kk % nj
        ss = (kk // nj) % n_seq
        ee = kk // (nj * n_seq)
        base = (ss * n_exp + ee) * capp + jj * tr

        def one(r):
            tok = idx_ref[base + r]
            return o_hbm.at[pl.ds(ss * seq_len + tok, 1), :], xbuf.at[sl, pl.ds(r, 1), :]

        return one

    def gather_start(kk, sl):
        one = row_pairs(kk, sl)

        def body(r, c):
            src, dst = one(r)
            pltpu.make_async_copy(src, dst, gsem.at[sl]).start()
            return c

        lax.fori_loop(0, tr, body, 0, unroll=8)

    def scatter_start(kk, sl):
        one = row_pairs(kk, sl)

        def body(r, c):
            dst, src = one(r)
            pltpu.make_async_copy(src, dst, ssem.at[sl]).start()
            return c

        lax.fori_loop(0, tr, body, 0, unroll=8)

    def gather_wait(sl):
        pltpu.make_async_copy(o_hbm.at[pl.ds(0, tr), :], xbuf.at[sl], gsem.at[sl]).wait()

    def scatter_wait(sl):
        pltpu.make_async_copy(xbuf.at[sl], o_hbm.at[pl.ds(0, tr), :], ssem.at[sl]).wait()

    @pl.when(k == 0)
    def _():
        gather_start(k, slot)

    gather_wait(slot)

    @pl.when(k + 1 < total)
    def _():
        @pl.when(k >= 1)
        def _():
            scatter_wait(1 - slot)

        gather_start(k + 1, 1 - slot)

    xbuf[slot] = xbuf[slot] + g_ref[0] * y_ref[0].astype(F32)
    scatter_start(k, slot)

    @pl.when(k == total - 1)
    def _():
        scatter_wait(slot)

        @pl.when(total > 1)
        def _():
            scatter_wait(1 - slot)


def _combine(x2d, y, idx, g2, seq_len, cap, row0):
    t, d = x2d.shape
    s, e, _, capp = idx.shape
    tr = _tile(cap, 256, 16)
    nj = cap // tr
    assert s * nj >= 2, "consecutive steps must touch disjoint rows"
    assert row0 % tr == 0 and t >= tr
    nrows = g2.shape[0]
    grid_spec = pltpu.PrefetchScalarGridSpec(
        num_scalar_prefetch=1,
        grid=(e, s, nj),
        in_specs=[
            pl.BlockSpec((1, tr, d), lambda ei, si, j, idx_r: (ei, row0 // tr + si * nj + j, 0)),
            pl.BlockSpec((1, 1, d), lambda ei, si, j, idx_r: (si if nrows > 1 else 0, 0, 0)),
            pl.BlockSpec(memory_space=pl.ANY),
        ],
        out_specs=pl.BlockSpec(memory_space=pl.ANY),
        scratch_shapes=[pltpu.VMEM((2, tr, d), F32), pltpu.SemaphoreType.DMA((2,)), pltpu.SemaphoreType.DMA((2,))],
    )
    return pl.pallas_call(
        functools.partial(_combine_kernel, seq_len=seq_len, capp=capp, tr=tr, n_seq=s, n_exp=e),
        out_shape=jax.ShapeDtypeStruct((t, d), F32),
        grid_spec=grid_spec,
        input_output_aliases={3: 0},
        compiler_params=_cparams(("arbitrary", "arbitrary", "arbitrary")),
        name="moe_combine",
    )(idx.reshape(-1), y, g2, x2d)


def _moe(srcs, gain, wr_hi, wr_lo, w_gate, w_up, w_down, layer):
    n_exp = wr_hi.shape[0]
    routed = []
    for x2d, shift, scale, _, seq_len in srcs:
        cap = CAPACITY_FACTOR * seq_len // n_exp
        hp, aff = _norm_router(x2d, shift, scale, gain, wr_hi, wr_lo, seq_len)
        lpad = -(-seq_len // (LANES * SUBLANES)) * (LANES * SUBLANES)
        if lpad != seq_len:
            aff = jnp.pad(aff, ((0, 0), (0, 0), (0, lpad - seq_len)), constant_values=-1.0)
        idx, gate = _topk(aff, cap)
        routed.append((hp, idx, gate, seq_len, cap))
    y = _expert_ffn(routed, w_gate, w_up, w_down, layer)
    outs, row0 = [], 0
    for (x2d, _, _, g2, seq_len), (_, idx, _, _, cap) in zip(srcs, routed):
        outs.append(_combine(x2d, y, idx, g2, seq_len, cap, row0))
        row0 += idx.shape[0] * cap
    return outs


def kernel(x, c, ctx, c_ctx, w_ada, b_ada, norm1, norm2, w_in, q_norm, k_norm, rpb, conv_w, sg_norm, sg_w,
           sg_b, w_out, w_router, w_gate, w_up, w_down):
    b, n, d = x.shape
    lc = ctx.shape[1]
    depth = w_ada.shape[0]
    na = d // 2
    cwid = d // 4
    swid = d - na - cwid
    in_cols = w_in.shape[2]
    assert in_cols == 3 * na + 3 * cwid + 2 * swid and b + 1 <= SUBLANES

    cvec = jnp.concatenate([c, c_ctx[None, :], jnp.zeros((SUBLANES - b - 1, d), F32)], axis=0)
    mod = _adaln(cvec, w_ada, b_ada).reshape(depth, SUBLANES, N_MOD, 1, d)

    xl = x.reshape(b * n, d)
    xc = ctx.reshape(b * lc, d)
    for l in range(depth):
        update_ctx = l < depth - 1
        lat = [mod[l, :b, k] for k in range(N_MOD)]
        cx = [mod[l, b:b + 1, k] for k in range(N_MOD)]
        w_in_bf = w_in[l].astype(BF16)
        w_out_bf = w_out[l].astype(BF16)
        sg_w_bf = sg_w[l].astype(BF16)
        wr_t = w_router[l].T
        wr_hi = wr_t.astype(BF16)
        wr_lo = (wr_t - wr_hi.astype(F32)).astype(BF16)
        bias = _bias_table(rpb[l], n // GRID_W)

        c_col0, c_ncols = (0, in_cols) if update_ctx else (na, 2 * na)
        pc = _inproj(xc, cx[0], cx[1], norm1[l], w_in_bf, q_norm[l], k_norm[l], lc,
                     col0=c_col0, ncols=c_ncols, na_width=na).reshape(b, lc, c_ncols)
        p = _inproj(xl, lat[0], lat[1], norm1[l], w_in_bf, q_norm[l], k_norm[l], n,
                    col0=0, ncols=in_cols, na_width=na).reshape(b, n, in_cols)

        a = _natten(p, pc, bias, na_width=na, kc_col=na - c_col0, vc_col=2 * na - c_col0)
        m = _mix(p, conv_w[l], sg_norm[l], sg_w_bf, sg_b[l], col0=3 * na, cwid=cwid, swid=swid)
        xl = _outproj(a.reshape(b * n, na), m.reshape(b * n, cwid + swid), w_out_bf, xl, lat[2], n)
        srcs = [(xl, lat[3], lat[4], lat[5], n)]
        if update_ctx:
            a_c = _ctx_attn(pc, na_width=na)
            m_c = _mix(pc, conv_w[l], sg_norm[l], sg_w_bf, sg_b[l], col0=3 * na, cwid=cwid, swid=swid)
            xc = _outproj(a_c.reshape(b * lc, na), m_c.reshape(b * lc, cwid + swid), w_out_bf, xc, cx[2], lc)
            srcs.append((xc, cx[3], cx[4], cx[5], lc))
        outs = _moe(srcs, norm2[l], wr_hi, wr_lo, w_gate, w_up, w_down, l)
        xl = outs[0]
        if update_ctx:
            xc = outs[1]
    return xl.reshape(b, n, d)
```

```python
import functools

import jax
import jax.numpy as jnp
from jax import lax
from jax.experimental import pallas as pl
from jax.experimental.pallas import tpu as pltpu

F32 = jnp.float32
BF16 = jnp.bfloat16

GRID_W = 64
HEAD_DIM = 128
NA_KH = 8
NA_KW = 16
CHUNK = 128
N_MOD = 6
CAPACITY_FACTOR = 2
EPS = 1e-6
NEG = -1e30

LANES = 128
SUBLANES = 8
VMEM_LIMIT = 56 * 1024 * 1024

ATT_ROWS = 4
ATT_KROWS = ATT_ROWS + NA_KH
ATT_HPG = 4
NORM_ROWS = 64
LOG2E = 1.4426950408889634


def _cparams(sem, vmem=VMEM_LIMIT):
    return pltpu.CompilerParams(dimension_semantics=sem, vmem_limit_bytes=vmem)


def _tile(n, pref, align):
    if n <= pref:
        return n
    t = (pref // align) * align
    while t >= align:
        if n % t == 0:
            return t
        t -= align
    raise ValueError(f"no tile for {n} (pref {pref}, align {align})")


def _dot(a, b):
    return jnp.dot(a, b, preferred_element_type=F32)


def _dot_nt(a, b):
    return lax.dot_general(a, b, (((1,), (1,)), ((), ())), preferred_element_type=F32)


def _silu(v):
    return v * jax.nn.sigmoid(v)


def _adaln_kernel(c_ref, w_ref, b_ref, o_ref):
    c = c_ref[...]
    s = _silu(c).astype(BF16)
    o_ref[0] = _dot(s, w_ref[0].astype(BF16)) + b_ref[0]


def _adaln(cvec, w_ada, b_ada):
    depth, d, cols = w_ada.shape
    rows = cvec.shape[0]
    tn = _tile(cols, 512, LANES)
    return pl.pallas_call(
        _adaln_kernel,
        out_shape=jax.ShapeDtypeStruct((depth, rows, cols), F32),
        grid=(depth, cols // tn),
        in_specs=[
            pl.BlockSpec((rows, d), lambda l, j: (0, 0)),
            pl.BlockSpec((1, d, tn), lambda l, j: (l, 0, j)),
            pl.BlockSpec((1, 1, tn), lambda l, j: (l, 0, j)),
        ],
        out_specs=pl.BlockSpec((1, rows, tn), lambda l, j: (l, 0, j)),
        compiler_params=_cparams(("parallel", "parallel")),
        name="adaln",
    )(cvec, w_ada, b_ada.reshape(depth, 1, cols))


def _cast_kernel(x_ref, o_ref):
    o_ref[...] = x_ref[...].astype(BF16)


def _cast_bf16(w):
    depth, r, c = w.shape
    tr = _tile(r, 512, 16)
    tc = _tile(c, 4096, LANES)
    return pl.pallas_call(
        _cast_kernel,
        out_shape=jax.ShapeDtypeStruct(w.shape, BF16),
        grid=(depth, r // tr, c // tc),
        in_specs=[pl.BlockSpec((1, tr, tc), lambda l, i, j: (l, i, j))],
        out_specs=pl.BlockSpec((1, tr, tc), lambda l, i, j: (l, i, j)),
        compiler_params=_cparams(("parallel", "parallel", "parallel")),
        name="cast_bf16",
    )(w)


def _att_cases(rows):
    nrb = rows // ATT_ROWS
    cases = []
    for rb in (0, min(1, nrb - 1), nrb - 1):
        r0 = rb * ATT_ROWS
        u = min(max(r0 - NA_KH // 2, 0), rows - ATT_KROWS)
        per_row = []
        for a in range(ATT_ROWS):
            r = r0 + a
            s = min(max(r - NA_KH // 2, 0), rows - NA_KH)
            per_row.append((s - u, s - r + NA_KH - 1))
        cases.append(per_row)
    return cases


def _bias_kernel(rpb_ref, o_ref, *, cases, n_dr, n_dc):
    h = pl.program_id(0)
    qc = lax.broadcasted_iota(jnp.int32, (GRID_W, GRID_W), 0)
    kc = lax.broadcasted_iota(jnp.int32, (GRID_W, GRID_W), 1)
    cs = jnp.clip(qc - NA_KW // 2, 0, GRID_W - NA_KW)
    col_ok = (kc >= cs) & (kc < cs + NA_KW)
    dc = kc - qc + NA_KW - 1
    neg = jnp.full((GRID_W, GRID_W), NEG, F32)

    def toeplitz(dr):
        t = neg
        for i in range(n_dc):
            t = jnp.where(dc == i, rpb_ref[h, dr, i] * LOG2E, t)
        return jnp.where(col_ok, t, neg)

    tiles = [toeplitz(dr) for dr in range(n_dr)]
    for ci, per_row in enumerate(cases):
        for a, (j_first, dr_first) in enumerate(per_row):
            for j in range(ATT_KROWS):
                inside = j_first <= j < j_first + NA_KH
                t = tiles[dr_first + (j - j_first)] if inside else neg
                o_ref[ci, 0, a * GRID_W:(a + 1) * GRID_W, j * GRID_W:(j + 1) * GRID_W] = t


def _bias_table(rpb_l, rows):
    h, n_dr, n_dc = rpb_l.shape
    cases = _att_cases(rows)
    rq, rk = ATT_ROWS * GRID_W, ATT_KROWS * GRID_W
    return pl.pallas_call(
        functools.partial(_bias_kernel, cases=cases, n_dr=n_dr, n_dc=n_dc),
        out_shape=jax.ShapeDtypeStruct((len(cases), h, rq, rk), F32),
        grid=(h,),
        in_specs=[pl.BlockSpec(memory_space=pltpu.SMEM)],
        out_specs=pl.BlockSpec((len(cases), 1, rq, rk), lambda hh: (0, hh, 0, 0)),
        compiler_params=_cparams(("parallel",)),
        name="rpb_bias",
    )(rpb_l)


def _inproj_kernel(x_hbm, sh_ref, sc_ref, g_ref, w_ref, qg_ref, kg_ref, o_ref, xbuf, h_scr, sem, *, j0, nq, tn, tm):
    i = pl.program_id(0)
    j = pl.program_id(1)

    def x_copy(tile):
        return pltpu.make_async_copy(x_hbm.at[pl.ds(tile * tm, tm), :], xbuf, sem)

    @pl.when((i == 0) & (j == 0))
    def _():
        x_copy(0).start()

    @pl.when(j == 0)
    def _():
        x_copy(i).wait()
        rc = NORM_ROWS if tm % NORM_ROWS == 0 else tm

        def chunk(c, carry):
            rows = pl.ds(pl.multiple_of(c * rc, rc), rc)
            xf = xbuf[rows, :]
            ms = jnp.mean(xf * xf, axis=-1, keepdims=True)
            y = xf * lax.rsqrt(ms + EPS) * g_ref[...]
            h_scr[rows, :] = (y * (1.0 + sc_ref[0]) + sh_ref[0]).astype(BF16)
            return carry

        lax.fori_loop(0, tm // rc, chunk, 0)

    @pl.when((j == 1) & (i + 1 < pl.num_programs(0)))
    def _():
        x_copy(i + 1).start()

    acc = _dot(h_scr[...], w_ref[0])
    jj = j + j0

    def head_norm(gain):
        for hh in range(tn // HEAD_DIM):
            sl = slice(hh * HEAD_DIM, (hh + 1) * HEAD_DIM)
            blk = acc[:, sl]
            ms = jnp.mean(blk * blk, axis=-1, keepdims=True)
            o_ref[:, sl] = (blk * lax.rsqrt(ms + EPS) * gain).astype(BF16)

    @pl.when(jj < nq)
    def _():
        head_norm(qg_ref[...] * (HEAD_DIM ** -0.5 * LOG2E))

    @pl.when((jj >= nq) & (jj < 2 * nq))
    def _():
        head_norm(kg_ref[...])

    @pl.when(jj >= 2 * nq)
    def _():
        o_ref[...] = acc.astype(BF16)


def _inproj(x2d, shift, scale, gain, w_bf, layer, q_gain, k_gain, seq_len, *, col0, ncols, na_width):
    t, d = x2d.shape
    tm = _tile(seq_len, 1024, 16)
    tn = _tile(na_width // 2, 1024, HEAD_DIM)
    assert col0 % tn == 0 and ncols % tn == 0 and t % tm == 0
    assert ncols // tn >= 2, "the next token tile is prefetched at column step 1"
    nrows = shift.shape[0]
    row = (lambda i: (i * tm) // seq_len) if nrows > 1 else (lambda i: 0)
    j0 = col0 // tn
    return pl.pallas_call(
        functools.partial(_inproj_kernel, j0=j0, nq=na_width // tn, tn=tn, tm=tm),
        out_shape=jax.ShapeDtypeStruct((t, ncols), BF16),
        grid=(t // tm, ncols // tn),
        in_specs=[
            pl.BlockSpec(memory_space=pl.ANY),
            pl.BlockSpec((1, 1, d), lambda i, j: (row(i), 0, 0)),
            pl.BlockSpec((1, 1, d), lambda i, j: (row(i), 0, 0)),
            pl.BlockSpec((1, d), lambda i, j: (0, 0)),
            pl.BlockSpec((1, d, tn), lambda i, j: (layer, 0, j + j0)),
            pl.BlockSpec((1, HEAD_DIM), lambda i, j: (0, 0)),
            pl.BlockSpec((1, HEAD_DIM), lambda i, j: (0, 0)),
        ],
        out_specs=pl.BlockSpec((tm, tn), lambda i, j: (i, j)),
        scratch_shapes=[pltpu.VMEM((tm, d), F32), pltpu.VMEM((tm, d), BF16), pltpu.SemaphoreType.DMA],
        compiler_params=_cparams(("arbitrary", "arbitrary")),
        name="norm_inproj",
    )(x2d, shift, scale, gain.reshape(1, d), w_bf, q_gain.reshape(1, HEAD_DIM), k_gain.reshape(1, HEAD_DIM))


def _natten_kernel(q_ref, k_ref, v_ref, kc_ref, vc_ref, b_ref, o_ref, *, rows):
    rb = pl.program_id(2)
    u = jnp.clip(rb * ATT_ROWS - NA_KH // 2, 0, rows - ATT_KROWS)
    tok0 = pl.multiple_of(u * GRID_W, GRID_W)
    nk = ATT_KROWS * GRID_W
    lc = kc_ref.shape[1]
    ones_w = jnp.ones((nk, HEAD_DIM), BF16)
    ones_c = jnp.ones((lc, HEAD_DIM), BF16)

    def lane_tiles(s):
        return [s[:, j * LANES:(j + 1) * LANES] for j in range(s.shape[1] // LANES)]

    for hh in range(ATT_HPG):
        sl = slice(hh * HEAD_DIM, (hh + 1) * HEAD_DIM)
        q = q_ref[0, :, sl]
        kw = k_ref[0, pl.ds(tok0, nk), sl]
        kc = kc_ref[0, :, sl]
        vw = jnp.concatenate([v_ref[0, pl.ds(tok0, nk), sl], ones_w], axis=1)
        vc = jnp.concatenate([vc_ref[0, :, sl], ones_c], axis=1)
        s_loc = _dot_nt(q, kw) + b_ref[0, hh]
        s_ctx = _dot_nt(q, kc)
        m = jnp.max(functools.reduce(jnp.maximum, lane_tiles(s_loc) + lane_tiles(s_ctx)), axis=-1, keepdims=True)
        p_loc = jnp.exp2(s_loc - m).astype(BF16)
        p_ctx = jnp.exp2(s_ctx - m).astype(BF16)
        o = _dot(p_loc, vw) + _dot(p_ctx, vc)
        o_ref[0, :, sl] = (o[:, :HEAD_DIM] / o[:, HEAD_DIM:]).astype(BF16)


def _natten(p3, pc3, bias, *, na_width, kc_col, vc_col):
    b, n, _ = p3.shape
    lc = pc3.shape[1]
    rows = n // GRID_W
    assert rows % ATT_ROWS == 0 and rows >= ATT_KROWS
    nrb = rows // ATT_ROWS
    cw = ATT_HPG * HEAD_DIM
    ng = na_width // cw
    rq = ATT_ROWS * GRID_W

    def case(rb):
        return jnp.where(rb == 0, 0, jnp.where(rb == nrb - 1, 2, 1))

    return pl.pallas_call(
        functools.partial(_natten_kernel, rows=rows),
        out_shape=jax.ShapeDtypeStruct((b, n, na_width), BF16),
        grid=(b, ng, nrb),
        in_specs=[
            pl.BlockSpec((1, rq, cw), lambda bi, g, rb: (bi, rb, g)),
            pl.BlockSpec((1, n, cw), lambda bi, g, rb: (bi, 0, ng + g)),
            pl.BlockSpec((1, n, cw), lambda bi, g, rb: (bi, 0, 2 * ng + g)),
            pl.BlockSpec((1, lc, cw), lambda bi, g, rb: (bi, 0, kc_col // cw + g)),
            pl.BlockSpec((1, lc, cw), lambda bi, g, rb: (bi, 0, vc_col // cw + g)),
            pl.BlockSpec((1, ATT_HPG, rq, ATT_KROWS * GRID_W), lambda bi, g, rb: (case(rb), g, 0, 0)),
        ],
        out_specs=pl.BlockSpec((1, rq, cw), lambda bi, g, rb: (bi, rb, g)),
        compiler_params=_cparams(("parallel", "parallel", "arbitrary")),
        name="natten",
    )(p3, p3, p3, pc3, pc3, bias)


def _ctx_attn_kernel(q_ref, k_ref, v_ref, o_ref):
    for hh in range(ATT_HPG):
        sl = slice(hh * HEAD_DIM, (hh + 1) * HEAD_DIM)
        s = _dot_nt(q_ref[0, :, sl], k_ref[0, :, sl])
        m = jnp.max(s, axis=-1, keepdims=True)
        p = jnp.exp2(s - m)
        den = jnp.sum(p, axis=-1, keepdims=True)
        o_ref[0, :, sl] = (_dot(p.astype(BF16), v_ref[0, :, sl]) / den).astype(BF16)


def _ctx_attn(pc3, *, na_width):
    b, lc, _ = pc3.shape
    cw = ATT_HPG * HEAD_DIM
    ng = na_width // cw
    return pl.pallas_call(
        _ctx_attn_kernel,
        out_shape=jax.ShapeDtypeStruct((b, lc, na_width), BF16),
        grid=(b, ng),
        in_specs=[
            pl.BlockSpec((1, lc, cw), lambda bi, g: (bi, 0, g)),
            pl.BlockSpec((1, lc, cw), lambda bi, g: (bi, 0, ng + g)),
            pl.BlockSpec((1, lc, cw), lambda bi, g: (bi, 0, 2 * ng + g)),
        ],
        out_specs=pl.BlockSpec((1, lc, cw), lambda bi, g: (bi, 0, g)),
        compiler_params=_cparams(("parallel", "parallel")),
        name="ctx_attn",
    )(pc3, pc3, pc3)


def _mix_kernel(xin_ref, gb_ref, gc_ref, su_ref, sv_ref, xp_ref, cp_ref, xn_ref, cn_ref,
                cw_ref, sgn_ref, sgw_ref, sgb_ref, o_ref, *, tm, cwid, groups, halo):
    i = pl.program_id(1)
    last = pl.num_programs(1) - 1
    z = gc_ref[0].astype(F32) * xin_ref[0].astype(F32)
    zp = cp_ref[0, halo - 1:halo, :].astype(F32) * xp_ref[0, halo - 1:halo, :].astype(F32)
    zn = cn_ref[0, 0:1, :].astype(F32) * xn_ref[0, 0:1, :].astype(F32)
    zp = jnp.where(i == 0, 0.0, zp)
    zn = jnp.where(i == last, 0.0, zn)
    row = lax.broadcasted_iota(jnp.int32, (tm, cwid), 0)
    z_m1 = jnp.where(row == 0, zp, pltpu.roll(z, 1, axis=0))
    z_p1 = jnp.where(row == tm - 1, zn, pltpu.roll(z, tm - 1, axis=0))
    conv = cw_ref[0:1, :] * z_m1 + cw_ref[1:2, :] * z + cw_ref[2:3, :] * z_p1
    o_ref[0, :, 0:cwid] = (gb_ref[0].astype(F32) * conv).astype(BF16)

    sv = sv_ref[0].astype(F32)
    ms = jnp.mean(sv * sv, axis=-1, keepdims=True)
    vn = (sv * lax.rsqrt(ms + EPS) * sgn_ref[...]).astype(BF16)
    for g in range(groups):
        gs = slice(g * LANES, (g + 1) * LANES)
        for c in range(tm // CHUNK):
            ts = slice(c * CHUNK, (c + 1) * CHUNK)
            mixed = _dot(sgw_ref[g], vn[ts, gs]) + sgb_ref[g]
            o_ref[0, ts, cwid + g * LANES:cwid + (g + 1) * LANES] = (
                su_ref[0, ts, gs].astype(F32) * mixed).astype(BF16)


def _mix(p3, conv_w, sg_norm, sg_w_bf, sg_b, *, col0, cwid, swid):
    b, l, _ = p3.shape
    groups = sg_w_bf.shape[0]
    assert swid == groups * LANES and cwid == swid and l % CHUNK == 0 and col0 % cwid == 0
    tm = _tile(l, 512, CHUNK)
    halo = 16
    nt = l // tm
    c0 = col0 // cwid
    hb = tm // halo
    nhb = l // halo

    def main(k):
        return pl.BlockSpec((1, tm, cwid), lambda bi, i: (bi, i, c0 + k))

    def prev(k):
        return pl.BlockSpec((1, halo, cwid), lambda bi, i: (bi, jnp.maximum(i * hb - 1, 0), c0 + k))

    def nxt(k):
        return pl.BlockSpec((1, halo, cwid), lambda bi, i: (bi, jnp.minimum((i + 1) * hb, nhb - 1), c0 + k))

    full = lambda shape: pl.BlockSpec(shape, lambda bi, i: (0,) * len(shape))
    return pl.pallas_call(
        functools.partial(_mix_kernel, tm=tm, cwid=cwid, groups=groups, halo=halo),
        out_shape=jax.ShapeDtypeStruct((b, l, cwid + swid), BF16),
        grid=(b, nt),
        in_specs=[main(0), main(1), main(2), main(3), main(4), prev(0), prev(2), nxt(0), nxt(2),
                  full((3, cwid)), full((1, swid)), full((groups, CHUNK, CHUNK)), full((groups, CHUNK, 1))],
        out_specs=pl.BlockSpec((1, tm, cwid + swid), lambda bi, i: (bi, i, 0)),
        compiler_params=_cparams(("parallel", "parallel")),
        name="conv_gmlp",
    )(p3, p3, p3, p3, p3, p3, p3, p3, p3, conv_w, sg_norm.reshape(1, swid), sg_w_bf,
      sg_b.reshape(groups, CHUNK, 1))


def _outproj_kernel(a_ref, m_ref, wa_ref, wm_ref, x_ref, g_ref, o_ref):
    acc = _dot(a_ref[...], wa_ref[0]) + _dot(m_ref[...], wm_ref[0])
    o_ref[...] = x_ref[...] + g_ref[0] * acc


def _outproj(a2d, m2d, w_bf, layer, x2d, gate, seq_len):
    t, d = x2d.shape
    ka, km = a2d.shape[1], m2d.shape[1]
    assert ka == km
    tm = _tile(seq_len, 1024, 16)
    tn = _tile(d, 1024, LANES)
    nrows = gate.shape[0]
    row = (lambda i: (i * tm) // seq_len) if nrows > 1 else (lambda i: 0)
    return pl.pallas_call(
        _outproj_kernel,
        out_shape=jax.ShapeDtypeStruct((t, d), F32),
        grid=(t // tm, d // tn),
        in_specs=[
            pl.BlockSpec((tm, ka), lambda i, j: (i, 0)),
            pl.BlockSpec((tm, km), lambda i, j: (i, 0)),
            pl.BlockSpec((1, ka, tn), lambda i, j: (layer, 0, j)),
            pl.BlockSpec((1, km, tn), lambda i, j: (layer, 1, j)),
            pl.BlockSpec((tm, tn), lambda i, j: (i, j)),
            pl.BlockSpec((1, 1, tn), lambda i, j: (row(i), 0, j)),
        ],
        out_specs=pl.BlockSpec((tm, tn), lambda i, j: (i, j)),
        compiler_params=_cparams(("parallel", "parallel")),
        name="outproj",
    )(a2d, m2d, w_bf, w_bf, x2d, gate)


def _norm_router_kernel(x_ref, sh_ref, sc_ref, g_ref, wh_ref, wl_ref, h_ref, aff_ref):
    xf = x_ref[...]
    ms = jnp.mean(xf * xf, axis=-1, keepdims=True)
    y = xf * lax.rsqrt(ms + EPS) * g_ref[...]
    h = y * (1.0 + sc_ref[0]) + sh_ref[0]
    h_hi = h.astype(BF16)
    half = h.shape[1] // 2
    lo = pltpu.bitcast(h_hi[:, :half].astype(F32), jnp.uint32) >> 16
    hi = pltpu.bitcast(h_hi[:, half:].astype(F32), jnp.uint32) & jnp.uint32(0xFFFF0000)
    h_ref[...] = hi | lo
    h_lo = (h - h_hi.astype(F32)).astype(BF16)
    logits = _dot_nt(wh_ref[...], h_hi) + _dot_nt(wh_ref[...], h_lo) + _dot_nt(wl_ref[...], h_hi)
    m = jnp.max(logits, axis=0, keepdims=True)
    e = jnp.exp(logits - m)
    aff_ref[0] = e / jnp.sum(e, axis=0, keepdims=True)


def _norm_router(x2d, shift, scale, gain, wr_hi, wr_lo, seq_len):
    t, d = x2d.shape
    e = wr_hi.shape[0]
    nseq = t // seq_len
    tm = _tile(seq_len, 512, LANES)
    per = seq_len // tm
    nrows = shift.shape[0]
    row = (lambda i: i // per) if nrows > 1 else (lambda i: 0)
    return pl.pallas_call(
        _norm_router_kernel,
        out_shape=(jax.ShapeDtypeStruct((t, d // 2), jnp.uint32), jax.ShapeDtypeStruct((nseq, e, seq_len), F32)),
        grid=(t // tm,),
        in_specs=[
            pl.BlockSpec((tm, d), lambda i: (i, 0)),
            pl.BlockSpec((1, 1, d), lambda i: (row(i), 0, 0)),
            pl.BlockSpec((1, 1, d), lambda i: (row(i), 0, 0)),
            pl.BlockSpec((1, d), lambda i: (0, 0)),
            pl.BlockSpec((e, d), lambda i: (0, 0)),
            pl.BlockSpec((e, d), lambda i: (0, 0)),
        ],
        out_specs=(pl.BlockSpec((tm, d // 2), lambda i: (i, 0)),
                   pl.BlockSpec((1, e, tm), lambda i: (i // per, 0, i % per))),
        compiler_params=_cparams(("parallel",)),
        name="norm_router",
    )(x2d, shift, scale, gain.reshape(1, d), wr_hi, wr_lo)


def _topk_kernel(aff_ref, idx_ref, gate_ref, thr_scr, *, nr, cap, capp, n_exp):
    bits_all = pltpu.bitcast(aff_ref[0], jnp.int32)

    def count_all(mask):
        c = jnp.sum(mask.astype(F32), axis=1, keepdims=True)
        return jnp.sum(c, axis=2, keepdims=True)

    def search(i, thr):
        cand = thr | jnp.left_shift(jnp.int32(1), 30 - i)
        return jnp.where(count_all(bits_all >= cand) >= cap, cand, thr)

    thr_scr[...] = lax.fori_loop(0, 31, search, jnp.zeros((n_exp, 1, 1), jnp.int32))

    def per_expert(e, carry):
        _topk_compact(aff_ref[0, e], thr_scr[e], idx_ref.at[0, e], gate_ref.at[0, e], nr=nr, cap=cap, capp=capp)
        return carry

    lax.fori_loop(0, n_exp, per_expert, 0)


def _topk_compact(aff, thr, idx_ref, gate_ref, *, nr, cap, capp):
    bits = pltpu.bitcast(aff, jnp.int32)

    def count(mask):
        c = jnp.sum(mask.astype(F32), axis=0, keepdims=True)
        return jnp.sum(c, axis=1, keepdims=True)

    gt = bits > thr
    eq = bits == thr
    need = cap - count(gt)

    lane_u = lax.broadcasted_iota(jnp.int32, (LANES, LANES), 0)
    lane_t = lax.broadcasted_iota(jnp.int32, (LANES, LANES), 1)
    tri_incl = (lane_u <= lane_t).astype(BF16)
    row_a = lax.broadcasted_iota(jnp.int32, (nr, nr), 0)
    row_b = lax.broadcasted_iota(jnp.int32, (nr, nr), 1)
    low_strict = (row_b < row_a).astype(BF16)

    def row_offsets(maskf):
        tot = jnp.sum(maskf, axis=1, keepdims=True)
        totb = jnp.broadcast_to(tot, (nr, LANES)).astype(BF16)
        return tot, _dot(low_strict, totb)[:, 0:1]

    eqf = eq.astype(F32)
    _, eq_off = row_offsets(eqf)
    eq_rank = _dot(eq.astype(BF16), tri_incl) - eqf + eq_off
    sel = gt | (eq & (eq_rank < need))
    self32 = sel.astype(F32)
    selb = sel.astype(BF16)

    tot, offx = row_offsets(self32)
    offi = offx + tot
    slot = lax.broadcasted_iota(jnp.int32, (1, capp), 1).astype(F32)
    r_of = jnp.sum((offi <= slot).astype(F32), axis=0, keepdims=True)
    rows_col = lax.broadcasted_iota(jnp.int32, (nr, capp), 0).astype(F32)
    onehot = rows_col == r_of
    onehot_b = onehot.astype(BF16)
    local = slot - jnp.sum(jnp.where(onehot, offx, 0.0), axis=0, keepdims=True)
    incl_t = _dot_nt((lane_t <= lane_u).astype(BF16), selb)
    pref = _dot(incl_t.astype(BF16), onehot_b)
    lane_of = jnp.sum((pref <= local).astype(F32), axis=0, keepdims=True)
    idx = r_of * LANES + lane_of
    idx_ref[...] = jnp.clip(idx, 0, nr * LANES - 1).astype(jnp.int32)

    eye = (lane_u == lane_t).astype(BF16)
    a_hi = aff.astype(BF16)
    r1 = aff - a_hi.astype(F32)
    a_mid = r1.astype(BF16)
    a_lo = (r1 - a_mid.astype(F32)).astype(BF16)
    rows_t = jnp.zeros((LANES, capp), F32)
    for part in (a_hi, a_mid, a_lo):
        part_t = _dot_nt(eye, part).astype(BF16)
        rows_t = rows_t + _dot(part_t, onehot_b)
    lane_col = lax.broadcasted_iota(jnp.int32, (LANES, capp), 0).astype(F32)
    gate_ref[...] = jnp.sum(jnp.where(lane_col == lane_of, rows_t, 0.0), axis=0, keepdims=True)


def _topk(aff, cap):
    s, e, l = aff.shape
    nr = l // LANES
    assert l % (LANES * SUBLANES) == 0
    capp = -(-cap // LANES) * LANES
    shp = (s, e, 1, capp)
    spec = pl.BlockSpec((1, e, 1, capp), lambda si: (si, 0, 0, 0))
    return pl.pallas_call(
        functools.partial(_topk_kernel, nr=nr, cap=cap, capp=capp, n_exp=e),
        out_shape=(jax.ShapeDtypeStruct(shp, jnp.int32), jax.ShapeDtypeStruct(shp, F32)),
        grid=(s,),
        in_specs=[pl.BlockSpec((1, e, nr, LANES), lambda si: (si, 0, 0, 0))],
        out_specs=(spec, spec),
        scratch_shapes=[pltpu.VMEM((e, 1, 1), jnp.int32)],
        compiler_params=_cparams(("parallel",)),
        name="expert_topk",
    )(aff.reshape(s, e, nr, LANES))


def _ffn_kernel(*refs, n_src, segs, n_exp, m_rows, lat_rows, per_step, rb, n_up, n_dn, fu, half):
    idx_refs = refs[:n_src]
    h_hbms = refs[n_src:2 * n_src]
    gate_ref, wg_ref, wu_ref, wd_ref, o_ref, hsp, act, sem = refs[2 * n_src:]
    e = pl.program_id(0)
    st = pl.program_id(1)
    hi_mask = jnp.uint32(0xFFFF0000)

    def issue_rows(ee, lo, hi):
        for row0, nrows, si, s, seq_len, capp in segs:
            a, b = max(lo, row0), min(hi, row0 + nrows)
            if a >= b:
                continue

            def body(r, c, row0=row0, si=si, s=s, seq_len=seq_len, capp=capp):
                tok = idx_refs[si][(s * n_exp + ee) * capp + (r - row0)]
                pltpu.make_async_copy(h_hbms[si].at[pl.ds(s * seq_len + tok, 1), :],
                                      hsp.at[pl.ds(r, 1), :], sem).start()
                return c

            lax.fori_loop(a, b, body, 0, unroll=8)

    def wait_rows():
        pltpu.make_async_copy(h_hbms[0].at[pl.ds(0, m_rows), :], hsp, sem).wait()

    @pl.when((e == 0) & (st == 0))
    def _():
        issue_rows(0, 0, m_rows)

    @pl.when(st == 0)
    def _():
        wait_rows()

    @pl.when(st < n_up)
    def _():
        w = jnp.concatenate([wg_ref[0, 0].astype(BF16), wu_ref[0, 0].astype(BF16)], axis=1)
        for rbi in range(m_rows // rb):
            rows = slice(rbi * rb, (rbi + 1) * rb)
            hp = hsp[rows, :]
            lo = pltpu.bitcast(hp << 16, F32).astype(BF16)
            hi = pltpu.bitcast(hp & hi_mask, F32).astype(BF16)
            gu = _dot(lo, w[:half]) + _dot(hi, w[half:])
            a = (_silu(gu[:, :fu]) * gu[:, fu:]).astype(BF16)
            for k in range(n_up):
                @pl.when(st == k)
                def _(k=k, a=a, rows=rows):
                    act[rows, k * fu:(k + 1) * fu] = a

    @pl.when(st >= n_up)
    def _():
        k = st - n_up
        ee = jnp.minimum(e + 1, n_exp - 1)
        _, cap0, _, _, seq_len0, capp0 = segs[0]
        per_seq = cap0 // per_step
        s0 = k // per_seq
        base_idx = (s0 * n_exp + ee) * capp0 + (k % per_seq) * per_step
        base_src = s0 * seq_len0
        base_dst = k * per_step

        @pl.when(k == 0)
        def _():
            issue_rows(ee, lat_rows, m_rows)

        wd = wd_ref[0, 0].astype(BF16)
        nblk = m_rows // rb
        for rbi in range(nblk):
            rows = slice(rbi * rb, (rbi + 1) * rb)
            y = _dot(act[rows, :], wd) * gate_ref[0, rows, :]
            o_ref[0, rows, :] = y.astype(BF16)
            for r in range((per_step * rbi) // nblk, (per_step * (rbi + 1)) // nblk):
                tok = idx_refs[0][base_idx + r]
                pltpu.make_async_copy(h_hbms[0].at[pl.ds(base_src + tok, 1), :],
                                      hsp.at[pl.ds(base_dst + r, 1), :], sem).start()

        @pl.when((e == n_exp - 1) & (k == n_dn - 1))
        def _():
            wait_rows()


def _expert_ffn(routed, w_gate, w_up, w_down, layer):
    _, n_exp, d, ff = w_gate.shape
    n_src = len(routed)
    segs, gates, row0 = [], [], 0
    for si, (hp, idx, gate, seq_len, cap) in enumerate(routed):
        n_seq, _, _, capp = idx.shape
        for s in range(n_seq):
            segs.append((row0, cap, si, s, seq_len, capp))
            row0 += cap
        gates.append(jnp.transpose(gate[:, :, 0, :cap], (1, 0, 2)).reshape(n_exp, n_seq * cap))
    m_rows = row0
    gate_all = jnp.concatenate(gates, axis=1).reshape(n_exp, m_rows, 1)
    rb = _tile(m_rows, -(-m_rows // 4), 16) if m_rows % 64 == 0 else m_rows
    fu = _tile(ff, 2 * LANES, LANES)
    dnw = _tile(d, 512, LANES)
    n_up, n_dn = ff // fu, d // dnw
    lat_rows = routed[0][1].shape[0] * routed[0][4]
    per_step = lat_rows // n_dn
    assert routed[0][0].shape[0] >= m_rows
    assert lat_rows % n_dn == 0 and routed[0][4] % per_step == 0

    def up_idx(ei, st, *_):
        return layer, ei, 0, jnp.minimum(st, n_up - 1)

    def dn_idx(ei, st, *_):
        return ei, 0, jnp.clip(st - n_up, 0, n_dn - 1)

    grid_spec = pltpu.PrefetchScalarGridSpec(
        num_scalar_prefetch=n_src,
        grid=(n_exp, n_up + n_dn),
        in_specs=[pl.BlockSpec(memory_space=pl.ANY)] * n_src + [
            pl.BlockSpec((1, m_rows, 1), lambda ei, st, *_: (ei, 0, 0)),
            pl.BlockSpec((1, 1, d, fu), up_idx),
            pl.BlockSpec((1, 1, d, fu), up_idx),
            pl.BlockSpec((1, 1, ff, dnw), lambda ei, st, *_: (layer,) + dn_idx(ei, st)),
        ],
        out_specs=pl.BlockSpec((1, m_rows, dnw), dn_idx),
        scratch_shapes=[pltpu.VMEM((m_rows, d // 2), jnp.uint32), pltpu.VMEM((m_rows, ff), BF16),
                        pltpu.SemaphoreType.DMA],
    )
    return pl.pallas_call(
        functools.partial(_ffn_kernel, n_src=n_src, segs=tuple(segs), n_exp=n_exp, m_rows=m_rows,
                          lat_rows=lat_rows, per_step=per_step, rb=rb, n_up=n_up, n_dn=n_dn, fu=fu, half=d // 2),
        out_shape=jax.ShapeDtypeStruct((n_exp, m_rows, d), BF16),
        grid_spec=grid_spec,
        compiler_params=_cparams(("arbitrary", "arbitrary")),
        name="expert_ffn",
    )(*[r[1].reshape(-1) for r in routed], *[r[0] for r in routed], gate_all, w_gate, w_up, w_down)


def _combine_kernel(idx_ref, y_ref, g_ref, x_hbm, o_hbm, xbuf, gsem, ssem, *, seq_len, capp, tr, n_seq, n_exp):
    del x_hbm
    e = pl.program_id(0)
    s = pl.program_id(1)
    j = pl.program_id(2)
    nj = pl.num_programs(2)
    k = (e * n_seq + s) * nj + j
    total = n_exp * n_seq * nj
    slot = k % 2

    def row_pairs(kk, sl):
        jj = kk % nj
        ss = (kk // nj) % n_seq
        ee = kk // (nj * n_seq)
        base = (ss * n_exp + ee) * capp + jj * tr

        def one(r):
            tok = idx_ref[base + r]
            return o_hbm.at[pl.ds(ss * seq_len + tok, 1), :], xbuf.at[sl, pl.ds(r, 1), :]

        return one

    def gather_start(kk, sl):
        one = row_pairs(kk, sl)

        def body(r, c):
            src, dst = one(r)
            pltpu.make_async_copy(src, dst, gsem.at[sl]).start()
            return c

        lax.fori_loop(0, tr, body, 0, unroll=8)

    def scatter_start(kk, sl):
        one = row_pairs(kk, sl)

        def body(r, c):
            dst, src = one(r)
            pltpu.make_async_copy(src, dst, ssem.at[sl]).start()
            return c

        lax.fori_loop(0, tr, body, 0, unroll=8)

    def gather_wait(sl):
        pltpu.make_async_copy(o_hbm.at[pl.ds(0, tr), :], xbuf.at[sl], gsem.at[sl]).wait()

    def scatter_wait(sl):
        pltpu.make_async_copy(xbuf.at[sl], o_hbm.at[pl.ds(0, tr), :], ssem.at[sl]).wait()

    @pl.when(k == 0)
    def _():
        gather_start(k, slot)

    gather_wait(slot)

    @pl.when(k + 1 < total)
    def _():
        @pl.when(k >= 1)
        def _():
            scatter_wait(1 - slot)

        gather_start(k + 1, 1 - slot)

    xbuf[slot] = xbuf[slot] + g_ref[0] * y_ref[0].astype(F32)
    scatter_start(k, slot)

    @pl.when(k == total - 1)
    def _():
        scatter_wait(slot)

        @pl.when(total > 1)
        def _():
            scatter_wait(1 - slot)


def _combine(x2d, y, idx, g2, seq_len, cap, row0):
    t, d = x2d.shape
    s, e, _, capp = idx.shape
    tr = _tile(cap, 256, 16)
    nj = cap // tr
    assert s * nj >= 2, "consecutive steps must touch disjoint rows"
    assert row0 % tr == 0 and t >= tr
    nrows = g2.shape[0]
    grid_spec = pltpu.PrefetchScalarGridSpec(
        num_scalar_prefetch=1,
        grid=(e, s, nj),
        in_specs=[
            pl.BlockSpec((1, tr, d), lambda ei, si, j, idx_r: (ei, row0 // tr + si * nj + j, 0)),
            pl.BlockSpec((1, 1, d), lambda ei, si, j, idx_r: (si if nrows > 1 else 0, 0, 0)),
            pl.BlockSpec(memory_space=pl.ANY),
        ],
        out_specs=pl.BlockSpec(memory_space=pl.ANY),
        scratch_shapes=[pltpu.VMEM((2, tr, d), F32), pltpu.SemaphoreType.DMA((2,)), pltpu.SemaphoreType.DMA((2,))],
    )
    return pl.pallas_call(
        functools.partial(_combine_kernel, seq_len=seq_len, capp=capp, tr=tr, n_seq=s, n_exp=e),
        out_shape=jax.ShapeDtypeStruct((t, d), F32),
        grid_spec=grid_spec,
        input_output_aliases={3: 0},
        compiler_params=_cparams(("arbitrary", "arbitrary", "arbitrary")),
        name="moe_combine",
    )(idx.reshape(-1), y, g2, x2d)


def _moe(srcs, gain, wr_hi, wr_lo, w_gate, w_up, w_down, layer):
    n_exp = wr_hi.shape[0]
    routed = []
    for x2d, shift, scale, _, seq_len in srcs:
        cap = CAPACITY_FACTOR * seq_len // n_exp
        hp, aff = _norm_router(x2d, shift, scale, gain, wr_hi, wr_lo, seq_len)
        lpad = -(-seq_len // (LANES * SUBLANES)) * (LANES * SUBLANES)
        if lpad != seq_len:
            aff = jnp.pad(aff, ((0, 0), (0, 0), (0, lpad - seq_len)), constant_values=-1.0)
        idx, gate = _topk(aff, cap)
        routed.append((hp, idx, gate, seq_len, cap))
    y = _expert_ffn(routed, w_gate, w_up, w_down, layer)
    outs, row0 = [], 0
    for (x2d, _, _, g2, seq_len), (_, idx, _, _, cap) in zip(srcs, routed):
        outs.append(_combine(x2d, y, idx, g2, seq_len, cap, row0))
        row0 += idx.shape[0] * cap
    return outs


def kernel(x, c, ctx, c_ctx, w_ada, b_ada, norm1, norm2, w_in, q_norm, k_norm, rpb, conv_w, sg_norm, sg_w,
           sg_b, w_out, w_router, w_gate, w_up, w_down):
    b, n, d = x.shape
    lc = ctx.shape[1]
    depth = w_ada.shape[0]
    na = d // 2
    cwid = d // 4
    swid = d - na - cwid
    in_cols = w_in.shape[2]
    assert in_cols == 3 * na + 3 * cwid + 2 * swid and b + 1 <= SUBLANES

    cvec = jnp.concatenate([c, c_ctx[None, :], jnp.zeros((SUBLANES - b - 1, d), F32)], axis=0)
    mod = _adaln(cvec, w_ada, b_ada).reshape(depth, SUBLANES, N_MOD, 1, d)

    w_in_bf = _cast_bf16(w_in)
    w_out_bf = _cast_bf16(w_out)
    xl = x.reshape(b * n, d)
    xc = ctx.reshape(b * lc, d)
    for l in range(depth):
        update_ctx = l < depth - 1
        lat = [mod[l, :b, k] for k in range(N_MOD)]
        cx = [mod[l, b:b + 1, k] for k in range(N_MOD)]
        sg_w_bf = sg_w[l].astype(BF16)
        wr_t = w_router[l].T
        wr_hi = wr_t.astype(BF16)
        wr_lo = (wr_t - wr_hi.astype(F32)).astype(BF16)
        bias = _bias_table(rpb[l], n // GRID_W)

        c_col0, c_ncols = (0, in_cols) if update_ctx else (na, 2 * na)
        pc = _inproj(xc, cx[0], cx[1], norm1[l], w_in_bf, l, q_norm[l], k_norm[l], lc,
                     col0=c_col0, ncols=c_ncols, na_width=na).reshape(b, lc, c_ncols)
        p = _inproj(xl, lat[0], lat[1], norm1[l], w_in_bf, l, q_norm[l], k_norm[l], n,
                    col0=0, ncols=in_cols, na_width=na).reshape(b, n, in_cols)

        a = _natten(p, pc, bias, na_width=na, kc_col=na - c_col0, vc_col=2 * na - c_col0)
        m = _mix(p, conv_w[l], sg_norm[l], sg_w_bf, sg_b[l], col0=3 * na, cwid=cwid, swid=swid)
        xl = _outproj(a.reshape(b * n, na), m.reshape(b * n, cwid + swid), w_out_bf, l, xl, lat[2], n)
        srcs = [(xl, lat[3], lat[4], lat[5], n)]
        if update_ctx:
            a_c = _ctx_attn(pc, na_width=na)
            m_c = _mix(pc, conv_w[l], sg_norm[l], sg_w_bf, sg_b[l], col0=3 * na, cwid=cwid, swid=swid)
            xc = _outproj(a_c.reshape(b * lc, na), m_c.reshape(b * lc, cwid + swid), w_out_bf, l, xc, cx[2], lc)
            srcs.append((xc, cx[3], cx[4], cx[5], lc))
        outs = _moe(srcs, norm2[l], wr_hi, wr_lo, w_gate, w_up, w_down, l)
        xl = outs[0]
        if update_ctx:
            xc = outs[1]
    return xl.reshape(b, n, d)
```

```python
import functools

import jax
import jax.numpy as jnp
from jax import lax
from jax.experimental import pallas as pl
from jax.experimental.pallas import tpu as pltpu

F32 = jnp.float32
BF16 = jnp.bfloat16

GRID_W = 64
HEAD_DIM = 128
NA_KH = 8
NA_KW = 16
CHUNK = 128
N_MOD = 6
CAPACITY_FACTOR = 2
EPS = 1e-6
NEG = -1e30

LANES = 128
SUBLANES = 8
VMEM_LIMIT = 56 * 1024 * 1024

ATT_ROWS = 4
ATT_KROWS = ATT_ROWS + NA_KH
ATT_HPG = 4
ATT_SUB = 2
NORM_ROWS = 64
LOG2E = 1.4426950408889634


def _cparams(sem, vmem=VMEM_LIMIT):
    return pltpu.CompilerParams(dimension_semantics=sem, vmem_limit_bytes=vmem)


def _tile(n, pref, align):
    if n <= pref:
        return n
    t = (pref // align) * align
    while t >= align:
        if n % t == 0:
            return t
        t -= align
    raise ValueError(f"no tile for {n} (pref {pref}, align {align})")


def _dot(a, b):
    return jnp.dot(a, b, preferred_element_type=F32)


def _dot_nt(a, b):
    return lax.dot_general(a, b, (((1,), (1,)), ((), ())), preferred_element_type=F32)


def _silu(v):
    return v * jax.nn.sigmoid(v)


def _adaln_kernel(c_ref, w_ref, b_ref, o_ref):
    c = c_ref[...]
    s = _silu(c).astype(BF16)
    o_ref[0] = _dot(s, w_ref[0].astype(BF16)) + b_ref[0]


def _adaln(cvec, w_ada, b_ada):
    depth, d, cols = w_ada.shape
    rows = cvec.shape[0]
    tn = _tile(cols, 512, LANES)
    return pl.pallas_call(
        _adaln_kernel,
        out_shape=jax.ShapeDtypeStruct((depth, rows, cols), F32),
        grid=(depth, cols // tn),
        in_specs=[
            pl.BlockSpec((rows, d), lambda l, j: (0, 0)),
            pl.BlockSpec((1, d, tn), lambda l, j: (l, 0, j)),
            pl.BlockSpec((1, 1, tn), lambda l, j: (l, 0, j)),
        ],
        out_specs=pl.BlockSpec((1, rows, tn), lambda l, j: (l, 0, j)),
        compiler_params=_cparams(("parallel", "parallel")),
        name="adaln",
    )(cvec, w_ada, b_ada.reshape(depth, 1, cols))


def _cast_kernel(x_ref, o_ref):
    o_ref[...] = x_ref[...].astype(BF16)


def _cast_bf16(w):
    depth, r, c = w.shape
    tr = _tile(r, 512, 16)
    tc = _tile(c, 4096, LANES)
    return pl.pallas_call(
        _cast_kernel,
        out_shape=jax.ShapeDtypeStruct(w.shape, BF16),
        grid=(depth, r // tr, c // tc),
        in_specs=[pl.BlockSpec((1, tr, tc), lambda l, i, j: (l, i, j))],
        out_specs=pl.BlockSpec((1, tr, tc), lambda l, i, j: (l, i, j)),
        compiler_params=_cparams(("parallel", "parallel", "parallel")),
        name="cast_bf16",
    )(w)


def _att_cases(rows):
    nrb = rows // ATT_ROWS
    cases = []
    for rb in (0, min(1, nrb - 1), nrb - 1):
        r0 = rb * ATT_ROWS
        u = min(max(r0 - NA_KH // 2, 0), rows - ATT_KROWS)
        per_row = []
        for a in range(ATT_ROWS):
            r = r0 + a
            s = min(max(r - NA_KH // 2, 0), rows - NA_KH)
            per_row.append((s - u, s - r + NA_KH - 1))
        cases.append(per_row)
    return cases


def _bias_kernel(rpb_ref, o_ref, *, cases, n_dr, n_dc):
    h = pl.program_id(0)
    qc = lax.broadcasted_iota(jnp.int32, (GRID_W, GRID_W), 0)
    kc = lax.broadcasted_iota(jnp.int32, (GRID_W, GRID_W), 1)
    cs = jnp.clip(qc - NA_KW // 2, 0, GRID_W - NA_KW)
    col_ok = (kc >= cs) & (kc < cs + NA_KW)
    dc = kc - qc + NA_KW - 1
    neg = jnp.full((GRID_W, GRID_W), NEG, F32)

    def toeplitz(dr):
        t = neg
        for i in range(n_dc):
            t = jnp.where(dc == i, rpb_ref[h, dr, i] * LOG2E, t)
        return jnp.where(col_ok, t, neg)

    tiles = [toeplitz(dr) for dr in range(n_dr)]
    for ci, per_row in enumerate(cases):
        for a, (j_first, dr_first) in enumerate(per_row):
            for j in range(ATT_KROWS):
                inside = j_first <= j < j_first + NA_KH
                t = tiles[dr_first + (j - j_first)] if inside else neg
                o_ref[ci, 0, a * GRID_W:(a + 1) * GRID_W, j * GRID_W:(j + 1) * GRID_W] = t


def _bias_table(rpb_l, rows):
    h, n_dr, n_dc = rpb_l.shape
    cases = _att_cases(rows)
    rq, rk = ATT_ROWS * GRID_W, ATT_KROWS * GRID_W
    return pl.pallas_call(
        functools.partial(_bias_kernel, cases=cases, n_dr=n_dr, n_dc=n_dc),
        out_shape=jax.ShapeDtypeStruct((len(cases), h, rq, rk), F32),
        grid=(h,),
        in_specs=[pl.BlockSpec(memory_space=pltpu.SMEM)],
        out_specs=pl.BlockSpec((len(cases), 1, rq, rk), lambda hh: (0, hh, 0, 0)),
        compiler_params=_cparams(("parallel",)),
        name="rpb_bias",
    )(rpb_l)


def _inproj_kernel(x_hbm, sh_ref, sc_ref, g_ref, w_ref, qg_ref, kg_ref, o_ref, xbuf, h_scr, sem, *, j0, nq, tn, tm):
    i = pl.program_id(0)
    j = pl.program_id(1)

    def x_copy(tile):
        return pltpu.make_async_copy(x_hbm.at[pl.ds(tile * tm, tm), :], xbuf, sem)

    @pl.when((i == 0) & (j == 0))
    def _():
        x_copy(0).start()

    @pl.when(j == 0)
    def _():
        x_copy(i).wait()
        rc = NORM_ROWS if tm % NORM_ROWS == 0 else tm
        gmod = g_ref[...] * (1.0 + sc_ref[0])

        def chunk(c, carry):
            rows = pl.ds(pl.multiple_of(c * rc, rc), rc)
            xf = xbuf[rows, :]
            ms = jnp.mean(xf * xf, axis=-1, keepdims=True)
            h_scr[rows, :] = (xf * lax.rsqrt(ms + EPS) * gmod + sh_ref[0]).astype(BF16)
            return carry

        lax.fori_loop(0, tm // rc, chunk, 0)

    @pl.when((j == 1) & (i + 1 < pl.num_programs(0)))
    def _():
        x_copy(i + 1).start()

    acc = _dot(h_scr[...], w_ref[0])
    jj = j + j0

    def head_norm(gain):
        for hh in range(tn // HEAD_DIM):
            sl = slice(hh * HEAD_DIM, (hh + 1) * HEAD_DIM)
            blk = acc[:, sl]
            ms = jnp.mean(blk * blk, axis=-1, keepdims=True)
            o_ref[:, sl] = (blk * lax.rsqrt(ms + EPS) * gain).astype(BF16)

    @pl.when(jj < nq)
    def _():
        head_norm(qg_ref[...] * (HEAD_DIM ** -0.5 * LOG2E))

    @pl.when((jj >= nq) & (jj < 2 * nq))
    def _():
        head_norm(kg_ref[...])

    @pl.when(jj >= 2 * nq)
    def _():
        o_ref[...] = acc.astype(BF16)


def _inproj(x2d, shift, scale, gain, w_bf, layer, q_gain, k_gain, seq_len, *, col0, ncols, na_width):
    t, d = x2d.shape
    nrows = shift.shape[0]
    tm = _tile(seq_len if nrows > 1 else t, 1024, 16)
    tn = _tile(na_width // 2, 1024, HEAD_DIM)
    assert col0 % tn == 0 and ncols % tn == 0 and t % tm == 0
    assert ncols // tn >= 2, "the next token tile is prefetched at column step 1"
    row = (lambda i: (i * tm) // seq_len) if nrows > 1 else (lambda i: 0)
    j0 = col0 // tn
    return pl.pallas_call(
        functools.partial(_inproj_kernel, j0=j0, nq=na_width // tn, tn=tn, tm=tm),
        out_shape=jax.ShapeDtypeStruct((t, ncols), BF16),
        grid=(t // tm, ncols // tn),
        in_specs=[
            pl.BlockSpec(memory_space=pl.ANY),
            pl.BlockSpec((1, 1, d), lambda i, j: (row(i), 0, 0)),
            pl.BlockSpec((1, 1, d), lambda i, j: (row(i), 0, 0)),
            pl.BlockSpec((1, d), lambda i, j: (0, 0)),
            pl.BlockSpec((1, d, tn), lambda i, j: (layer, 0, j + j0)),
            pl.BlockSpec((1, HEAD_DIM), lambda i, j: (0, 0)),
            pl.BlockSpec((1, HEAD_DIM), lambda i, j: (0, 0)),
        ],
        out_specs=pl.BlockSpec((tm, tn), lambda i, j: (i, j)),
        scratch_shapes=[pltpu.VMEM((tm, d), F32), pltpu.VMEM((tm, d), BF16), pltpu.SemaphoreType.DMA],
        compiler_params=_cparams(("arbitrary", "arbitrary")),
        name="norm_inproj",
    )(x2d, shift, scale, gain.reshape(1, d), w_bf, q_gain.reshape(1, HEAD_DIM), k_gain.reshape(1, HEAD_DIM))


def _natten_kernel(q_ref, k_ref, v_ref, kc_ref, vc_ref, *rest, rows):
    b_refs, o_ref = rest[:ATT_SUB], rest[ATT_SUB]
    nk = ATT_KROWS * GRID_W
    rq = ATT_ROWS * GRID_W
    lc = kc_ref.shape[1]
    ones_w = jnp.ones((nk, HEAD_DIM), BF16)
    ones_c = jnp.ones((lc, HEAD_DIM), BF16)

    def lane_tiles(s):
        return [s[:, j * LANES:(j + 1) * LANES] for j in range(s.shape[1] // LANES)]

    for sb in range(ATT_SUB):
        rb = pl.program_id(2) * ATT_SUB + sb
        u = jnp.clip(rb * ATT_ROWS - NA_KH // 2, 0, rows - ATT_KROWS)
        tok0 = pl.multiple_of(u * GRID_W, GRID_W)
        qrows = slice(sb * rq, (sb + 1) * rq)
        for hh in range(ATT_HPG):
            sl = slice(hh * HEAD_DIM, (hh + 1) * HEAD_DIM)
            q = q_ref[0, qrows, sl]
            kw = k_ref[0, pl.ds(tok0, nk), sl]
            kc = kc_ref[0, :, sl]
            vw = jnp.concatenate([v_ref[0, pl.ds(tok0, nk), sl], ones_w], axis=1)
            vc = jnp.concatenate([vc_ref[0, :, sl], ones_c], axis=1)
            s_loc = _dot_nt(q, kw) + b_refs[sb][0, hh]
            s_ctx = _dot_nt(q, kc)
            m = jnp.max(functools.reduce(jnp.maximum, lane_tiles(s_loc) + lane_tiles(s_ctx)), axis=-1, keepdims=True)
            p_loc = jnp.exp2(s_loc - m).astype(BF16)
            p_ctx = jnp.exp2(s_ctx - m).astype(BF16)
            o = _dot(p_loc, vw) + _dot(p_ctx, vc)
            o_ref[0, qrows, sl] = (o[:, :HEAD_DIM] / o[:, HEAD_DIM:]).astype(BF16)


def _natten(p3, pc3, bias, *, na_width, kc_col, vc_col):
    b, n, _ = p3.shape
    lc = pc3.shape[1]
    rows = n // GRID_W
    assert rows % (ATT_ROWS * ATT_SUB) == 0 and rows >= ATT_KROWS
    nrb = rows // ATT_ROWS
    cw = ATT_HPG * HEAD_DIM
    ng = na_width // cw
    rq = ATT_ROWS * GRID_W

    def bias_spec(sb):
        def idx(bi, g, st):
            rb = st * ATT_SUB + sb
            return jnp.where(rb == 0, 0, jnp.where(rb == nrb - 1, 2, 1)), g, 0, 0
        return pl.BlockSpec((1, ATT_HPG, rq, ATT_KROWS * GRID_W), idx)

    return pl.pallas_call(
        functools.partial(_natten_kernel, rows=rows),
        out_shape=jax.ShapeDtypeStruct((b, n, na_width), BF16),
        grid=(b, ng, nrb // ATT_SUB),
        in_specs=[
            pl.BlockSpec((1, ATT_SUB * rq, cw), lambda bi, g, st: (bi, st, g)),
            pl.BlockSpec((1, n, cw), lambda bi, g, st: (bi, 0, ng + g)),
            pl.BlockSpec((1, n, cw), lambda bi, g, st: (bi, 0, 2 * ng + g)),
            pl.BlockSpec((1, lc, cw), lambda bi, g, st: (bi, 0, kc_col // cw + g)),
            pl.BlockSpec((1, lc, cw), lambda bi, g, st: (bi, 0, vc_col // cw + g)),
        ] + [bias_spec(sb) for sb in range(ATT_SUB)],
        out_specs=pl.BlockSpec((1, ATT_SUB * rq, cw), lambda bi, g, st: (bi, st, g)),
        compiler_params=_cparams(("parallel", "parallel", "arbitrary")),
        name="natten",
    )(p3, p3, p3, pc3, pc3, *([bias] * ATT_SUB))


def _ctx_attn_kernel(q_ref, k_ref, v_ref, o_ref):
    for hh in range(ATT_HPG):
        sl = slice(hh * HEAD_DIM, (hh + 1) * HEAD_DIM)
        s = _dot_nt(q_ref[0, :, sl], k_ref[0, :, sl])
        m = jnp.max(s, axis=-1, keepdims=True)
        p = jnp.exp2(s - m)
        den = jnp.sum(p, axis=-1, keepdims=True)
        o_ref[0, :, sl] = (_dot(p.astype(BF16), v_ref[0, :, sl]) / den).astype(BF16)


def _ctx_attn(pc3, *, na_width):
    b, lc, _ = pc3.shape
    cw = ATT_HPG * HEAD_DIM
    ng = na_width // cw
    return pl.pallas_call(
        _ctx_attn_kernel,
        out_shape=jax.ShapeDtypeStruct((b, lc, na_width), BF16),
        grid=(b, ng),
        in_specs=[
            pl.BlockSpec((1, lc, cw), lambda bi, g: (bi, 0, g)),
            pl.BlockSpec((1, lc, cw), lambda bi, g: (bi, 0, ng + g)),
            pl.BlockSpec((1, lc, cw), lambda bi, g: (bi, 0, 2 * ng + g)),
        ],
        out_specs=pl.BlockSpec((1, lc, cw), lambda bi, g: (bi, 0, g)),
        compiler_params=_cparams(("parallel", "parallel")),
        name="ctx_attn",
    )(pc3, pc3, pc3)


def _mix_kernel(xin_ref, gb_ref, gc_ref, su_ref, sv_ref, xp_ref, cp_ref, xn_ref, cn_ref,
                cw_ref, sgn_ref, sgw_ref, sgb_ref, o_ref, *, tm, cwid, groups, halo):
    i = pl.program_id(1)
    last = pl.num_programs(1) - 1
    z = gc_ref[0].astype(F32) * xin_ref[0].astype(F32)
    zp = cp_ref[0, halo - 1:halo, :].astype(F32) * xp_ref[0, halo - 1:halo, :].astype(F32)
    zn = cn_ref[0, 0:1, :].astype(F32) * xn_ref[0, 0:1, :].astype(F32)
    zp = jnp.where(i == 0, 0.0, zp)
    zn = jnp.where(i == last, 0.0, zn)
    row = lax.broadcasted_iota(jnp.int32, (tm, cwid), 0)
    z_m1 = jnp.where(row == 0, zp, pltpu.roll(z, 1, axis=0))
    z_p1 = jnp.where(row == tm - 1, zn, pltpu.roll(z, tm - 1, axis=0))
    conv = cw_ref[0:1, :] * z_m1 + cw_ref[1:2, :] * z + cw_ref[2:3, :] * z_p1
    o_ref[0, :, 0:cwid] = (gb_ref[0].astype(F32) * conv).astype(BF16)

    sv = sv_ref[0].astype(F32)
    ms = jnp.mean(sv * sv, axis=-1, keepdims=True)
    vn = (sv * lax.rsqrt(ms + EPS) * sgn_ref[...]).astype(BF16)
    for g in range(groups):
        gs = slice(g * LANES, (g + 1) * LANES)
        for c in range(tm // CHUNK):
            ts = slice(c * CHUNK, (c + 1) * CHUNK)
            mixed = _dot(sgw_ref[g], vn[ts, gs]) + sgb_ref[g]
            o_ref[0, ts, cwid + g * LANES:cwid + (g + 1) * LANES] = (
                su_ref[0, ts, gs].astype(F32) * mixed).astype(BF16)


def _mix(p3, conv_w, sg_norm, sg_w_bf, sg_b, *, col0, cwid, swid):
    b, l, _ = p3.shape
    groups = sg_w_bf.shape[0]
    assert swid == groups * LANES and cwid == swid and l % CHUNK == 0 and col0 % cwid == 0
    tm = _tile(l, 512, CHUNK)
    halo = 16
    nt = l // tm
    c0 = col0 // cwid
    hb = tm // halo
    nhb = l // halo

    def main(k):
        return pl.BlockSpec((1, tm, cwid), lambda bi, i: (bi, i, c0 + k))

    def prev(k):
        return pl.BlockSpec((1, halo, cwid), lambda bi, i: (bi, jnp.maximum(i * hb - 1, 0), c0 + k))

    def nxt(k):
        return pl.BlockSpec((1, halo, cwid), lambda bi, i: (bi, jnp.minimum((i + 1) * hb, nhb - 1), c0 + k))

    full = lambda shape: pl.BlockSpec(shape, lambda bi, i: (0,) * len(shape))
    return pl.pallas_call(
        functools.partial(_mix_kernel, tm=tm, cwid=cwid, groups=groups, halo=halo),
        out_shape=jax.ShapeDtypeStruct((b, l, cwid + swid), BF16),
        grid=(b, nt),
        in_specs=[main(0), main(1), main(2), main(3), main(4), prev(0), prev(2), nxt(0), nxt(2),
                  full((3, cwid)), full((1, swid)), full((groups, CHUNK, CHUNK)), full((groups, CHUNK, 1))],
        out_specs=pl.BlockSpec((1, tm, cwid + swid), lambda bi, i: (bi, i, 0)),
        compiler_params=_cparams(("parallel", "parallel")),
        name="conv_gmlp",
    )(p3, p3, p3, p3, p3, p3, p3, p3, p3, conv_w, sg_norm.reshape(1, swid), sg_w_bf,
      sg_b.reshape(groups, CHUNK, 1))


def _outproj_kernel(a_ref, m_ref, wa_ref, wm_ref, x_ref, g_ref, o_ref):
    acc = _dot(a_ref[...], wa_ref[0]) + _dot(m_ref[...], wm_ref[0])
    o_ref[...] = x_ref[...] + g_ref[0] * acc


def _outproj(a2d, m2d, w_bf, layer, x2d, gate, seq_len):
    t, d = x2d.shape
    ka, km = a2d.shape[1], m2d.shape[1]
    assert ka == km
    tm = _tile(seq_len, 1024, 16)
    tn = _tile(d, 1024, LANES)
    nrows = gate.shape[0]
    row = (lambda i: (i * tm) // seq_len) if nrows > 1 else (lambda i: 0)
    return pl.pallas_call(
        _outproj_kernel,
        out_shape=jax.ShapeDtypeStruct((t, d), F32),
        grid=(t // tm, d // tn),
        in_specs=[
            pl.BlockSpec((tm, ka), lambda i, j: (i, 0)),
            pl.BlockSpec((tm, km), lambda i, j: (i, 0)),
            pl.BlockSpec((1, ka, tn), lambda i, j: (layer, 0, j)),
            pl.BlockSpec((1, km, tn), lambda i, j: (layer, 1, j)),
            pl.BlockSpec((tm, tn), lambda i, j: (i, j)),
            pl.BlockSpec((1, 1, tn), lambda i, j: (row(i), 0, j)),
        ],
        out_specs=pl.BlockSpec((tm, tn), lambda i, j: (i, j)),
        compiler_params=_cparams(("parallel", "parallel")),
        name="outproj",
    )(a2d, m2d, w_bf, w_bf, x2d, gate)


def _norm_router_kernel(x_ref, sh_ref, sc_ref, g_ref, wh_ref, wl_ref, h_ref, aff_ref):
    xf = x_ref[...]
    ms = jnp.mean(xf * xf, axis=-1, keepdims=True)
    h = xf * lax.rsqrt(ms + EPS) * (g_ref[...] * (1.0 + sc_ref[0])) + sh_ref[0]
    half = h.shape[1] // 2
    h_hi = h.astype(BF16)
    h_hi32 = h_hi.astype(F32)
    lo = lax.shift_right_logical(pltpu.bitcast(h_hi32[:, :half], jnp.int32), 16)
    hi = pltpu.bitcast(h_hi32[:, half:], jnp.int32) & jnp.int32(-65536)
    h_ref[...] = hi | lo
    h_lo = (h - h_hi32).astype(BF16)
    logits = _dot_nt(wh_ref[...], h_hi) + _dot_nt(wh_ref[...], h_lo) + _dot_nt(wl_ref[...], h_hi)
    m = jnp.max(logits, axis=0, keepdims=True)
    e = jnp.exp(logits - m)
    aff_ref[0] = e / jnp.sum(e, axis=0, keepdims=True)


def _norm_router(x2d, shift, scale, gain, wr_hi, wr_lo, seq_len):
    t, d = x2d.shape
    e = wr_hi.shape[0]
    nseq = t // seq_len
    tm = _tile(seq_len, 512, LANES)
    per = seq_len // tm
    nrows = shift.shape[0]
    row = (lambda i: i // per) if nrows > 1 else (lambda i: 0)
    return pl.pallas_call(
        _norm_router_kernel,
        out_shape=(jax.ShapeDtypeStruct((t, d // 2), jnp.int32), jax.ShapeDtypeStruct((nseq, e, seq_len), F32)),
        grid=(t // tm,),
        in_specs=[
            pl.BlockSpec((tm, d), lambda i: (i, 0)),
            pl.BlockSpec((1, 1, d), lambda i: (row(i), 0, 0)),
            pl.BlockSpec((1, 1, d), lambda i: (row(i), 0, 0)),
            pl.BlockSpec((1, d), lambda i: (0, 0)),
            pl.BlockSpec((e, d), lambda i: (0, 0)),
            pl.BlockSpec((e, d), lambda i: (0, 0)),
        ],
        out_specs=(pl.BlockSpec((tm, d // 2), lambda i: (i, 0)),
                   pl.BlockSpec((1, e, tm), lambda i: (i // per, 0, i % per))),
        compiler_params=_cparams(("parallel",)),
        name="norm_router",
    )(x2d, shift, scale, gain.reshape(1, d), wr_hi, wr_lo)


def _topk_kernel(aff_ref, idx_ref, gate_ref, thr_scr, *, nr, cap, capp, n_exp):
    bits_all = pltpu.bitcast(aff_ref[0], jnp.int32)

    def count_all(mask):
        c = jnp.sum(mask.astype(F32), axis=1, keepdims=True)
        return jnp.sum(c, axis=2, keepdims=True)

    def search(i, thr):
        cand = thr | jnp.left_shift(jnp.int32(1), 30 - i)
        return jnp.where(count_all(bits_all >= cand) >= cap, cand, thr)

    thr_scr[...] = lax.fori_loop(0, 31, search, jnp.zeros((n_exp, 1, 1), jnp.int32))

    def per_expert(e, carry):
        _topk_compact(aff_ref[0, e], thr_scr[e], idx_ref.at[0, e], gate_ref.at[0, e], nr=nr, cap=cap, capp=capp)
        return carry

    lax.fori_loop(0, n_exp, per_expert, 0)


def _topk_compact(aff, thr, idx_ref, gate_ref, *, nr, cap, capp):
    bits = pltpu.bitcast(aff, jnp.int32)

    def count(mask):
        c = jnp.sum(mask.astype(F32), axis=0, keepdims=True)
        return jnp.sum(c, axis=1, keepdims=True)

    gt = bits > thr
    eq = bits == thr
    need = cap - count(gt)

    lane_u = lax.broadcasted_iota(jnp.int32, (LANES, LANES), 0)
    lane_t = lax.broadcasted_iota(jnp.int32, (LANES, LANES), 1)
    tri_incl = (lane_u <= lane_t).astype(BF16)
    row_a = lax.broadcasted_iota(jnp.int32, (nr, nr), 0)
    row_b = lax.broadcasted_iota(jnp.int32, (nr, nr), 1)
    low_strict = (row_b < row_a).astype(BF16)

    def row_offsets(maskf):
        tot = jnp.sum(maskf, axis=1, keepdims=True)
        totb = jnp.broadcast_to(tot, (nr, LANES)).astype(BF16)
        return tot, _dot(low_strict, totb)[:, 0:1]

    eqf = eq.astype(F32)
    _, eq_off = row_offsets(eqf)
    eq_rank = _dot(eq.astype(BF16), tri_incl) - eqf + eq_off
    sel = gt | (eq & (eq_rank < need))
    self32 = sel.astype(F32)
    selb = sel.astype(BF16)

    tot, offx = row_offsets(self32)
    offi = offx + tot
    slot = lax.broadcasted_iota(jnp.int32, (1, capp), 1).astype(F32)
    r_of = jnp.sum((offi <= slot).astype(F32), axis=0, keepdims=True)
    rows_col = lax.broadcasted_iota(jnp.int32, (nr, capp), 0).astype(F32)
    onehot = rows_col == r_of
    onehot_b = onehot.astype(BF16)
    local = slot - jnp.sum(jnp.where(onehot, offx, 0.0), axis=0, keepdims=True)
    incl_t = _dot_nt((lane_t <= lane_u).astype(BF16), selb)
    pref = _dot(incl_t.astype(BF16), onehot_b)
    lane_of = jnp.sum((pref <= local).astype(F32), axis=0, keepdims=True)
    idx = r_of * LANES + lane_of
    idx_ref[...] = jnp.clip(idx, 0, nr * LANES - 1).astype(jnp.int32)

    eye = (lane_u == lane_t).astype(BF16)
    a_hi = aff.astype(BF16)
    r1 = aff - a_hi.astype(F32)
    a_mid = r1.astype(BF16)
    a_lo = (r1 - a_mid.astype(F32)).astype(BF16)
    rows_t = jnp.zeros((LANES, capp), F32)
    for part in (a_hi, a_mid, a_lo):
        part_t = _dot_nt(eye, part).astype(BF16)
        rows_t = rows_t + _dot(part_t, onehot_b)
    lane_col = lax.broadcasted_iota(jnp.int32, (LANES, capp), 0).astype(F32)
    gate_ref[...] = jnp.sum(jnp.where(lane_col == lane_of, rows_t, 0.0), axis=0, keepdims=True)


def _topk(aff, cap):
    s, e, l = aff.shape
    nr = l // LANES
    assert l % (LANES * SUBLANES) == 0
    capp = -(-cap // LANES) * LANES
    shp = (s, e, 1, capp)
    spec = pl.BlockSpec((1, e, 1, capp), lambda si: (si, 0, 0, 0))
    return pl.pallas_call(
        functools.partial(_topk_kernel, nr=nr, cap=cap, capp=capp, n_exp=e),
        out_shape=(jax.ShapeDtypeStruct(shp, jnp.int32), jax.ShapeDtypeStruct(shp, F32)),
        grid=(s,),
        in_specs=[pl.BlockSpec((1, e, nr, LANES), lambda si: (si, 0, 0, 0))],
        out_specs=(spec, spec),
        scratch_shapes=[pltpu.VMEM((e, 1, 1), jnp.int32)],
        compiler_params=_cparams(("parallel",)),
        name="expert_topk",
    )(aff.reshape(s, e, nr, LANES))


def _ffn_kernel(*refs, n_src, segs, n_exp, m_rows, lat_rows, per_step, rb, n_up, n_dn, fu, half):
    idx_refs = refs[:n_src]
    h_hbms = refs[n_src:2 * n_src]
    gate_ref, wg_ref, wu_ref, wd_ref, o_ref, hsp, act, sem = refs[2 * n_src:]
    e = pl.program_id(0)
    st = pl.program_id(1)
    hi_mask = jnp.int32(-65536)

    def issue_rows(ee, lo, hi):
        for row0, nrows, si, s, seq_len, capp in segs:
            a, b = max(lo, row0), min(hi, row0 + nrows)
            if a >= b:
                continue

            def body(r, c, row0=row0, si=si, s=s, seq_len=seq_len, capp=capp):
                tok = idx_refs[si][(s * n_exp + ee) * capp + (r - row0)]
                pltpu.make_async_copy(h_hbms[si].at[pl.ds(s * seq_len + tok, 1), :],
                                      hsp.at[pl.ds(r, 1), :], sem).start()
                return c

            lax.fori_loop(a, b, body, 0, unroll=8)

    def wait_rows():
        pltpu.make_async_copy(h_hbms[0].at[pl.ds(0, m_rows), :], hsp, sem).wait()

    @pl.when((e == 0) & (st == 0))
    def _():
        issue_rows(0, 0, m_rows)

    @pl.when(st == 0)
    def _():
        wait_rows()

    @pl.when(st < n_up)
    def _():
        w = jnp.concatenate([wg_ref[0, 0].astype(BF16), wu_ref[0, 0].astype(BF16)], axis=1)
        for rbi in range(m_rows // rb):
            rows = slice(rbi * rb, (rbi + 1) * rb)
            hp = hsp[rows, :]
            lo = pltpu.bitcast(hp << 16, F32).astype(BF16)
            hi = pltpu.bitcast(hp & hi_mask, F32).astype(BF16)
            gu = _dot(lo, w[:half]) + _dot(hi, w[half:])
            a = (_silu(gu[:, :fu]) * gu[:, fu:]).astype(BF16)
            for k in range(n_up):
                @pl.when(st == k)
                def _(k=k, a=a, rows=rows):
                    act[rows, k * fu:(k + 1) * fu] = a

    @pl.when(st >= n_up)
    def _():
        k = st - n_up
        ee = jnp.minimum(e + 1, n_exp - 1)
        _, cap0, _, _, seq_len0, capp0 = segs[0]
        per_seq = cap0 // per_step
        s0 = k // per_seq
        base_idx = (s0 * n_exp + ee) * capp0 + (k % per_seq) * per_step
        base_src = s0 * seq_len0
        base_dst = k * per_step

        @pl.when(k == 0)
        def _():
            issue_rows(ee, lat_rows, m_rows)

        wd = wd_ref[0, 0].astype(BF16)
        nblk = m_rows // rb
        for rbi in range(nblk):
            rows = slice(rbi * rb, (rbi + 1) * rb)
            y = _dot(act[rows, :], wd) * gate_ref[0, rows, :]
            o_ref[0, rows, :] = y.astype(BF16)
            for r in range((per_step * rbi) // nblk, (per_step * (rbi + 1)) // nblk):
                tok = idx_refs[0][base_idx + r]
                pltpu.make_async_copy(h_hbms[0].at[pl.ds(base_src + tok, 1), :],
                                      hsp.at[pl.ds(base_dst + r, 1), :], sem).start()

        @pl.when((e == n_exp - 1) & (k == n_dn - 1))
        def _():
            wait_rows()


def _expert_ffn(routed, w_gate, w_up, w_down, layer):
    _, n_exp, d, ff = w_gate.shape
    n_src = len(routed)
    segs, gates, row0 = [], [], 0
    for si, (hp, idx, gate, seq_len, cap) in enumerate(routed):
        n_seq, _, _, capp = idx.shape
        for s in range(n_seq):
            segs.append((row0, cap, si, s, seq_len, capp))
            row0 += cap
        gates.append(jnp.transpose(gate[:, :, 0, :cap], (1, 0, 2)).reshape(n_exp, n_seq * cap))
    m_rows = row0
    gate_all = jnp.concatenate(gates, axis=1).reshape(n_exp, m_rows, 1)
    rb = _tile(m_rows, -(-m_rows // 4), 16) if m_rows % 64 == 0 else m_rows
    fu = _tile(ff, 2 * LANES, LANES)
    dnw = _tile(d, 512, LANES)
    n_up, n_dn = ff // fu, d // dnw
    lat_rows = routed[0][1].shape[0] * routed[0][4]
    per_step = lat_rows // n_dn
    assert routed[0][0].shape[0] >= m_rows
    assert lat_rows % n_dn == 0 and routed[0][4] % per_step == 0

    def up_idx(ei, st, *_):
        return layer, ei, 0, jnp.minimum(st, n_up - 1)

    def dn_idx(ei, st, *_):
        return ei, 0, jnp.clip(st - n_up, 0, n_dn - 1)

    grid_spec = pltpu.PrefetchScalarGridSpec(
        num_scalar_prefetch=n_src,
        grid=(n_exp, n_up + n_dn),
        in_specs=[pl.BlockSpec(memory_space=pl.ANY)] * n_src + [
            pl.BlockSpec((1, m_rows, 1), lambda ei, st, *_: (ei, 0, 0)),
            pl.BlockSpec((1, 1, d, fu), up_idx),
            pl.BlockSpec((1, 1, d, fu), up_idx),
            pl.BlockSpec((1, 1, ff, dnw), lambda ei, st, *_: (layer,) + dn_idx(ei, st)),
        ],
        out_specs=pl.BlockSpec((1, m_rows, dnw), dn_idx),
        scratch_shapes=[pltpu.VMEM((m_rows, d // 2), jnp.int32), pltpu.VMEM((m_rows, ff), BF16),
                        pltpu.SemaphoreType.DMA],
    )
    return pl.pallas_call(
        functools.partial(_ffn_kernel, n_src=n_src, segs=tuple(segs), n_exp=n_exp, m_rows=m_rows,
                          lat_rows=lat_rows, per_step=per_step, rb=rb, n_up=n_up, n_dn=n_dn, fu=fu, half=d // 2),
        out_shape=jax.ShapeDtypeStruct((n_exp, m_rows, d), BF16),
        grid_spec=grid_spec,
        compiler_params=_cparams(("arbitrary", "arbitrary")),
        name="expert_ffn",
    )(*[r[1].reshape(-1) for r in routed], *[r[0] for r in routed], gate_all, w_gate, w_up, w_down)


def _combine_kernel(idx_ref, y_ref, g_ref, x_hbm, o_hbm, xbuf, gsem, ssem, *, seq_len, capp, tr, n_seq, n_exp):
    del x_hbm
    e = pl.program_id(0)
    s = pl.program_id(1)
    j = pl.program_id(2)
    nj = pl.num_programs(2)
    k = (e * n_seq + s) * nj + j
    total = n_exp * n_seq * nj
    slot = k % 2

    def row_pairs(kk, sl):
        jj = kk % nj
        ss = (kk // nj) % n_seq
        ee = kk // (nj * n_seq)
        base = (ss * n_exp + ee) * capp + jj * tr

        def one(r):
            tok = idx_ref[base + r]
            return o_hbm.at[pl.ds(ss * seq_len + tok, 1), :], xbuf.at[sl, pl.ds(r, 1), :]

        return one

    def gather_start(kk, sl):
        one = row_pairs(kk, sl)

        def body(r, c):
            src, dst = one(r)
            pltpu.make_async_copy(src, dst, gsem.at[sl]).start()
            return c

        lax.fori_loop(0, tr, body, 0, unroll=8)

    def scatter_start(kk, sl):
        one = row_pairs(kk, sl)

        def body(r, c):
            dst, src = one(r)
            pltpu.make_async_copy(src, dst, ssem.at[sl]).start()
            return c

        lax.fori_loop(0, tr, body, 0, unroll=8)

    def gather_wait(sl):
        pltpu.make_async_copy(o_hbm.at[pl.ds(0, tr), :], xbuf.at[sl], gsem.at[sl]).wait()

    def scatter_wait(sl):
        pltpu.make_async_copy(xbuf.at[sl], o_hbm.at[pl.ds(0, tr), :], ssem.at[sl]).wait()

    @pl.when(k == 0)
    def _():
        gather_start(k, slot)

    gather_wait(slot)

    @pl.when(k + 1 < total)
    def _():
        @pl.when(k >= 1)
        def _():
            scatter_wait(1 - slot)

        gather_start(k + 1, 1 - slot)

    xbuf[slot] = xbuf[slot] + g_ref[0] * y_ref[0].astype(F32)
    scatter_start(k, slot)

    @pl.when(k == total - 1)
    def _():
        scatter_wait(slot)

        @pl.when(total > 1)
        def _():
            scatter_wait(1 - slot)


def _combine(x2d, y, idx, g2, seq_len, cap, row0):
    t, d = x2d.shape
    s, e, _, capp = idx.shape
    tr = _tile(cap, 512, 16)
    nj = cap // tr
    assert s * nj >= 2, "consecutive steps must touch disjoint rows"
    assert row0 % tr == 0 and t >= tr
    nrows = g2.shape[0]
    grid_spec = pltpu.PrefetchScalarGridSpec(
        num_scalar_prefetch=1,
        grid=(e, s, nj),
        in_specs=[
            pl.BlockSpec((1, tr, d), lambda ei, si, j, idx_r: (ei, row0 // tr + si * nj + j, 0)),
            pl.BlockSpec((1, 1, d), lambda ei, si, j, idx_r: (si if nrows > 1 else 0, 0, 0)),
            pl.BlockSpec(memory_space=pl.ANY),
        ],
        out_specs=pl.BlockSpec(memory_space=pl.ANY),
        scratch_shapes=[pltpu.VMEM((2, tr, d), F32), pltpu.SemaphoreType.DMA((2,)), pltpu.SemaphoreType.DMA((2,))],
    )
    return pl.pallas_call(
        functools.partial(_combine_kernel, seq_len=seq_len, capp=capp, tr=tr, n_seq=s, n_exp=e),
        out_shape=jax.ShapeDtypeStruct((t, d), F32),
        grid_spec=grid_spec,
        input_output_aliases={3: 0},
        compiler_params=_cparams(("arbitrary", "arbitrary", "arbitrary")),
        name="moe_combine",
    )(idx.reshape(-1), y, g2, x2d)


def _moe(srcs, gain, wr_hi, wr_lo, w_gate, w_up, w_down, layer):
    n_exp = wr_hi.shape[0]
    routed = []
    for x2d, shift, scale, _, seq_len in srcs:
        cap = CAPACITY_FACTOR * seq_len // n_exp
        hp, aff = _norm_router(x2d, shift, scale, gain, wr_hi, wr_lo, seq_len)
        lpad = -(-seq_len // (LANES * SUBLANES)) * (LANES * SUBLANES)
        if lpad != seq_len:
            aff = jnp.pad(aff, ((0, 0), (0, 0), (0, lpad - seq_len)), constant_values=-1.0)
        idx, gate = _topk(aff, cap)
        routed.append((hp, idx, gate, seq_len, cap))
    y = _expert_ffn(routed, w_gate, w_up, w_down, layer)
    outs, row0 = [], 0
    for (x2d, _, _, g2, seq_len), (_, idx, _, _, cap) in zip(srcs, routed):
        outs.append(_combine(x2d, y, idx, g2, seq_len, cap, row0))
        row0 += idx.shape[0] * cap
    return outs


def kernel(x, c, ctx, c_ctx, w_ada, b_ada, norm1, norm2, w_in, q_norm, k_norm, rpb, conv_w, sg_norm, sg_w,
           sg_b, w_out, w_router, w_gate, w_up, w_down):
    b, n, d = x.shape
    lc = ctx.shape[1]
    depth = w_ada.shape[0]
    na = d // 2
    cwid = d // 4
    swid = d - na - cwid
    in_cols = w_in.shape[2]
    assert in_cols == 3 * na + 3 * cwid + 2 * swid and b + 1 <= SUBLANES

    cvec = jnp.concatenate([c, c_ctx[None, :], jnp.zeros((SUBLANES - b - 1, d), F32)], axis=0)
    mod = _adaln(cvec, w_ada, b_ada).reshape(depth, SUBLANES, N_MOD, 1, d)

    w_in_bf = _cast_bf16(w_in)
    w_out_bf = _cast_bf16(w_out)
    xl = x.reshape(b * n, d)
    xc = ctx.reshape(b * lc, d)
    for l in range(depth):
        update_ctx = l < depth - 1
        lat = [mod[l, :b, k] for k in range(N_MOD)]
        cx = [mod[l, b:b + 1, k] for k in range(N_MOD)]
        sg_w_bf = sg_w[l].astype(BF16)
        wr_t = w_router[l].T
        wr_hi = wr_t.astype(BF16)
        wr_lo = (wr_t - wr_hi.astype(F32)).astype(BF16)
        bias = _bias_table(rpb[l], n // GRID_W)

        c_col0, c_ncols = (0, in_cols) if update_ctx else (na, 2 * na)
        pc = _inproj(xc, cx[0], cx[1], norm1[l], w_in_bf, l, q_norm[l], k_norm[l], lc,
                     col0=c_col0, ncols=c_ncols, na_width=na).reshape(b, lc, c_ncols)
        p = _inproj(xl, lat[0], lat[1], norm1[l], w_in_bf, l, q_norm[l], k_norm[l], n,
                    col0=0, ncols=in_cols, na_width=na).reshape(b, n, in_cols)

        a = _natten(p, pc, bias, na_width=na, kc_col=na - c_col0, vc_col=2 * na - c_col0)
        m = _mix(p, conv_w[l], sg_norm[l], sg_w_bf, sg_b[l], col0=3 * na, cwid=cwid, swid=swid)
        xl = _outproj(a.reshape(b * n, na), m.reshape(b * n, cwid + swid), w_out_bf, l, xl, lat[2], n)
        srcs = [(xl, lat[3], lat[4], lat[5], n)]
        if update_ctx:
            a_c = _ctx_attn(pc, na_width=na)
            m_c = _mix(pc, conv_w[l], sg_norm[l], sg_w_bf, sg_b[l], col0=3 * na, cwid=cwid, swid=swid)
            xc = _outproj(a_c.reshape(b * lc, na), m_c.reshape(b * lc, cwid + swid), w_out_bf, l, xc, cx[2], lc)
            srcs.append((xc, cx[3], cx[4], cx[5], lc))
        outs = _moe(srcs, norm2[l], wr_hi, wr_lo, w_gate, w_up, w_down, l)
        xl = outs[0]
        if update_ctx:
            xc = outs[1]
    return xl.reshape(b, n, d)
```

```python
import functools

import jax
import jax.numpy as jnp
from jax import lax
from jax.experimental import pallas as pl
from jax.experimental.pallas import tpu as pltpu

F32 = jnp.float32
BF16 = jnp.bfloat16

GRID_W = 64
HEAD_DIM = 128
NA_KH = 8
NA_KW = 16
CHUNK = 128
N_MOD = 6
CAPACITY_FACTOR = 2
EPS = 1e-6
NEG = -1e30

LANES = 128
SUBLANES = 8
VMEM_LIMIT = 56 * 1024 * 1024

ATT_ROWS = 4
ATT_KROWS = ATT_ROWS + NA_KH
ATT_HPG = 4
ATT_SUB = 2
NORM_ROWS = 64
NORM_COLS = 512
LOG2E = 1.4426950408889634


def _cparams(sem, vmem=VMEM_LIMIT):
    return pltpu.CompilerParams(dimension_semantics=sem, vmem_limit_bytes=vmem)


def _tile(n, pref, align):
    if n <= pref:
        return n
    t = (pref // align) * align
    while t >= align:
        if n % t == 0:
            return t
        t -= align
    raise ValueError(f"no tile for {n} (pref {pref}, align {align})")


def _dot(a, b):
    return jnp.dot(a, b, preferred_element_type=F32)


def _dot_nt(a, b):
    return lax.dot_general(a, b, (((1,), (1,)), ((), ())), preferred_element_type=F32)


def _silu(v):
    return v * jax.nn.sigmoid(v)


def _adaln_kernel(c_ref, w_ref, b_ref, o_ref):
    c = c_ref[...]
    s = _silu(c).astype(BF16)
    o_ref[0] = _dot(s, w_ref[0].astype(BF16)) + b_ref[0]


def _adaln(cvec, w_ada, b_ada):
    depth, d, cols = w_ada.shape
    rows = cvec.shape[0]
    tn = _tile(cols, 512, LANES)
    return pl.pallas_call(
        _adaln_kernel,
        out_shape=jax.ShapeDtypeStruct((depth, rows, cols), F32),
        grid=(depth, cols // tn),
        in_specs=[
            pl.BlockSpec((rows, d), lambda l, j: (0, 0)),
            pl.BlockSpec((1, d, tn), lambda l, j: (l, 0, j)),
            pl.BlockSpec((1, 1, tn), lambda l, j: (l, 0, j)),
        ],
        out_specs=pl.BlockSpec((1, rows, tn), lambda l, j: (l, 0, j)),
        compiler_params=_cparams(("parallel", "parallel")),
        name="adaln",
    )(cvec, w_ada, b_ada.reshape(depth, 1, cols))


def _cast_kernel(x_ref, o_ref):
    o_ref[...] = x_ref[...].astype(BF16)


def _cast_bf16(w):
    depth, r, c = w.shape
    tr = _tile(r, 512, 16)
    tc = _tile(c, 4096, LANES)
    return pl.pallas_call(
        _cast_kernel,
        out_shape=jax.ShapeDtypeStruct(w.shape, BF16),
        grid=(depth, r // tr, c // tc),
        in_specs=[pl.BlockSpec((1, tr, tc), lambda l, i, j: (l, i, j))],
        out_specs=pl.BlockSpec((1, tr, tc), lambda l, i, j: (l, i, j)),
        compiler_params=_cparams(("parallel", "parallel", "parallel")),
        name="cast_bf16",
    )(w)


def _att_cases(rows):
    nrb = rows // ATT_ROWS
    cases = []
    for rb in (0, min(1, nrb - 1), nrb - 1):
        r0 = rb * ATT_ROWS
        u = min(max(r0 - NA_KH // 2, 0), rows - ATT_KROWS)
        per_row = []
        for a in range(ATT_ROWS):
            r = r0 + a
            s = min(max(r - NA_KH // 2, 0), rows - NA_KH)
            per_row.append((s - u, s - r + NA_KH - 1))
        cases.append(per_row)
    return cases


def _bias_kernel(rpb_ref, o_ref, *, cases, n_dr, n_dc):
    h = pl.program_id(0)
    qc = lax.broadcasted_iota(jnp.int32, (GRID_W, GRID_W), 0)
    kc = lax.broadcasted_iota(jnp.int32, (GRID_W, GRID_W), 1)
    cs = jnp.clip(qc - NA_KW // 2, 0, GRID_W - NA_KW)
    col_ok = (kc >= cs) & (kc < cs + NA_KW)
    dc = kc - qc + NA_KW - 1
    neg = jnp.full((GRID_W, GRID_W), NEG, F32)

    def toeplitz(dr):
        t = neg
        for i in range(n_dc):
            t = jnp.where(dc == i, rpb_ref[h, dr, i] * LOG2E, t)
        return jnp.where(col_ok, t, neg)

    tiles = [toeplitz(dr) for dr in range(n_dr)]
    for ci, per_row in enumerate(cases):
        for a, (j_first, dr_first) in enumerate(per_row):
            for j in range(ATT_KROWS):
                inside = j_first <= j < j_first + NA_KH
                t = tiles[dr_first + (j - j_first)] if inside else neg
                o_ref[ci, 0, a * GRID_W:(a + 1) * GRID_W, j * GRID_W:(j + 1) * GRID_W] = t


def _bias_table(rpb_l, rows):
    h, n_dr, n_dc = rpb_l.shape
    cases = _att_cases(rows)
    rq, rk = ATT_ROWS * GRID_W, ATT_KROWS * GRID_W
    return pl.pallas_call(
        functools.partial(_bias_kernel, cases=cases, n_dr=n_dr, n_dc=n_dc),
        out_shape=jax.ShapeDtypeStruct((len(cases), h, rq, rk), F32),
        grid=(h,),
        in_specs=[pl.BlockSpec(memory_space=pltpu.SMEM)],
        out_specs=pl.BlockSpec((len(cases), 1, rq, rk), lambda hh: (0, hh, 0, 0)),
        compiler_params=_cparams(("parallel",)),
        name="rpb_bias",
    )(rpb_l)


def _inproj_kernel(x_hbm, sh_ref, sc_ref, g_ref, w_ref, qg_ref, kg_ref, o_ref, xbuf, h_scr, rs_scr, sem,
                   *, j0, nq, tn, tm):
    i = pl.program_id(0)
    j = pl.program_id(1)

    def x_copy(tile):
        return pltpu.make_async_copy(x_hbm.at[pl.ds(tile * tm, tm), :], xbuf, sem)

    @pl.when((i == 0) & (j == 0))
    def _():
        x_copy(0).start()

    @pl.when(j == 0)
    def _():
        x_copy(i).wait()
        rc = NORM_ROWS if tm % NORM_ROWS == 0 else tm

        def row_scale(c, carry):
            rows = pl.ds(pl.multiple_of(c * rc, rc), rc)
            xf = xbuf[rows, :]
            ms = jnp.mean(xf * xf, axis=-1, keepdims=True)
            rs_scr[rows, :] = jnp.broadcast_to(lax.rsqrt(ms + EPS), (rc, LANES))
            return carry

        def modulate(c, carry):
            rows = pl.ds(pl.multiple_of(c * rc, rc), rc)
            rs = rs_scr[rows, :][:, 0:1]
            cb = NORM_COLS if xbuf.shape[1] % NORM_COLS == 0 else xbuf.shape[1]
            for c0 in range(0, xbuf.shape[1], cb):
                cols = slice(c0, c0 + cb)
                gmod = g_ref[:, cols] * (1.0 + sc_ref[0, :, cols])
                h_scr[rows, cols] = (xbuf[rows, cols] * rs * gmod + sh_ref[0, :, cols]).astype(BF16)
            return carry

        lax.fori_loop(0, tm // rc, row_scale, 0, unroll=2)
        lax.fori_loop(0, tm // rc, modulate, 0, unroll=2)

    @pl.when((j == 1) & (i + 1 < pl.num_programs(0)))
    def _():
        x_copy(i + 1).start()

    acc = _dot(h_scr[...], w_ref[0])
    jj = j + j0

    def head_norm(gain):
        for hh in range(tn // HEAD_DIM):
            sl = slice(hh * HEAD_DIM, (hh + 1) * HEAD_DIM)
            blk = acc[:, sl]
            ms = jnp.mean(blk * blk, axis=-1, keepdims=True)
            o_ref[:, sl] = (blk * lax.rsqrt(ms + EPS) * gain).astype(BF16)

    @pl.when(jj < nq)
    def _():
        head_norm(qg_ref[...] * (HEAD_DIM ** -0.5 * LOG2E))

    @pl.when((jj >= nq) & (jj < 2 * nq))
    def _():
        head_norm(kg_ref[...])

    @pl.when(jj >= 2 * nq)
    def _():
        o_ref[...] = acc.astype(BF16)


def _inproj(x2d, shift, scale, gain, w_bf, layer, q_gain, k_gain, seq_len, *, col0, ncols, na_width):
    t, d = x2d.shape
    nrows = shift.shape[0]
    tm = _tile(seq_len if nrows > 1 else t, 1024, 16)
    tn = _tile(na_width // 2, 1024, HEAD_DIM)
    assert col0 % tn == 0 and ncols % tn == 0 and t % tm == 0
    assert ncols // tn >= 2, "the next token tile is prefetched at column step 1"
    row = (lambda i: (i * tm) // seq_len) if nrows > 1 else (lambda i: 0)
    j0 = col0 // tn
    return pl.pallas_call(
        functools.partial(_inproj_kernel, j0=j0, nq=na_width // tn, tn=tn, tm=tm),
        out_shape=jax.ShapeDtypeStruct((t, ncols), BF16),
        grid=(t // tm, ncols // tn),
        in_specs=[
            pl.BlockSpec(memory_space=pl.ANY),
            pl.BlockSpec((1, 1, d), lambda i, j: (row(i), 0, 0)),
            pl.BlockSpec((1, 1, d), lambda i, j: (row(i), 0, 0)),
            pl.BlockSpec((1, d), lambda i, j: (0, 0)),
            pl.BlockSpec((1, d, tn), lambda i, j: (layer, 0, j + j0)),
            pl.BlockSpec((1, HEAD_DIM), lambda i, j: (0, 0)),
            pl.BlockSpec((1, HEAD_DIM), lambda i, j: (0, 0)),
        ],
        out_specs=pl.BlockSpec((tm, tn), lambda i, j: (i, j)),
        scratch_shapes=[pltpu.VMEM((tm, d), F32), pltpu.VMEM((tm, d), BF16), pltpu.VMEM((tm, LANES), F32),
                        pltpu.SemaphoreType.DMA],
        compiler_params=_cparams(("arbitrary", "arbitrary")),
        name="norm_inproj",
    )(x2d, shift, scale, gain.reshape(1, d), w_bf, q_gain.reshape(1, HEAD_DIM), k_gain.reshape(1, HEAD_DIM))


def _natten_kernel(q_ref, k_ref, v_ref, kc_ref, vc_ref, *rest, rows):
    b_refs, o_ref = rest[:ATT_SUB], rest[ATT_SUB]
    nk = ATT_KROWS * GRID_W
    rq = ATT_ROWS * GRID_W
    lc = kc_ref.shape[1]
    ones_w = jnp.ones((nk, HEAD_DIM), BF16)
    ones_c = jnp.ones((lc, HEAD_DIM), BF16)

    def lane_tiles(s):
        return [s[:, j * LANES:(j + 1) * LANES] for j in range(s.shape[1] // LANES)]

    for sb in range(ATT_SUB):
        rb = pl.program_id(2) * ATT_SUB + sb
        u = jnp.clip(rb * ATT_ROWS - NA_KH // 2, 0, rows - ATT_KROWS)
        tok0 = pl.multiple_of(u * GRID_W, GRID_W)
        qrows = slice(sb * rq, (sb + 1) * rq)
        for hh in range(ATT_HPG):
            sl = slice(hh * HEAD_DIM, (hh + 1) * HEAD_DIM)
            q = q_ref[0, qrows, sl]
            kw = k_ref[0, pl.ds(tok0, nk), sl]
            kc = kc_ref[0, :, sl]
            vw = jnp.concatenate([v_ref[0, pl.ds(tok0, nk), sl], ones_w], axis=1)
            vc = jnp.concatenate([vc_ref[0, :, sl], ones_c], axis=1)
            s_loc = _dot_nt(q, kw) + b_refs[sb][0, hh]
            s_ctx = _dot_nt(q, kc)
            m = jnp.max(functools.reduce(jnp.maximum, lane_tiles(s_loc) + lane_tiles(s_ctx)), axis=-1, keepdims=True)
            p_loc = jnp.exp2(s_loc - m).astype(BF16)
            p_ctx = jnp.exp2(s_ctx - m).astype(BF16)
            o = _dot(p_loc, vw) + _dot(p_ctx, vc)
            o_ref[0, qrows, sl] = (o[:, :HEAD_DIM] / o[:, HEAD_DIM:]).astype(BF16)


def _natten(p3, pc3, bias, *, na_width, kc_col, vc_col):
    b, n, _ = p3.shape
    lc = pc3.shape[1]
    rows = n // GRID_W
    assert rows % (ATT_ROWS * ATT_SUB) == 0 and rows >= ATT_KROWS
    nrb = rows // ATT_ROWS
    cw = ATT_HPG * HEAD_DIM
    ng = na_width // cw
    rq = ATT_ROWS * GRID_W

    def bias_spec(sb):
        def idx(bi, g, st):
            rb = st * ATT_SUB + sb
            return jnp.where(rb == 0, 0, jnp.where(rb == nrb - 1, 2, 1)), g, 0, 0
        return pl.BlockSpec((1, ATT_HPG, rq, ATT_KROWS * GRID_W), idx)

    return pl.pallas_call(
        functools.partial(_natten_kernel, rows=rows),
        out_shape=jax.ShapeDtypeStruct((b, n, na_width), BF16),
        grid=(b, ng, nrb // ATT_SUB),
        in_specs=[
            pl.BlockSpec((1, ATT_SUB * rq, cw), lambda bi, g, st: (bi, st, g)),
            pl.BlockSpec((1, n, cw), lambda bi, g, st: (bi, 0, ng + g)),
            pl.BlockSpec((1, n, cw), lambda bi, g, st: (bi, 0, 2 * ng + g)),
            pl.BlockSpec((1, lc, cw), lambda bi, g, st: (bi, 0, kc_col // cw + g)),
            pl.BlockSpec((1, lc, cw), lambda bi, g, st: (bi, 0, vc_col // cw + g)),
        ] + [bias_spec(sb) for sb in range(ATT_SUB)],
        out_specs=pl.BlockSpec((1, ATT_SUB * rq, cw), lambda bi, g, st: (bi, st, g)),
        compiler_params=_cparams(("parallel", "parallel", "arbitrary")),
        name="natten",
    )(p3, p3, p3, pc3, pc3, *([bias] * ATT_SUB))


def _ctx_attn_kernel(q_ref, k_ref, v_ref, o_ref):
    for hh in range(ATT_HPG):
        sl = slice(hh * HEAD_DIM, (hh + 1) * HEAD_DIM)
        s = _dot_nt(q_ref[0, :, sl], k_ref[0, :, sl])
        m = jnp.max(s, axis=-1, keepdims=True)
        p = jnp.exp2(s - m)
        den = jnp.sum(p, axis=-1, keepdims=True)
        o_ref[0, :, sl] = (_dot(p.astype(BF16), v_ref[0, :, sl]) / den).astype(BF16)


def _ctx_attn(pc3, *, na_width):
    b, lc, _ = pc3.shape
    cw = ATT_HPG * HEAD_DIM
    ng = na_width // cw
    return pl.pallas_call(
        _ctx_attn_kernel,
        out_shape=jax.ShapeDtypeStruct((b, lc, na_width), BF16),
        grid=(b, ng),
        in_specs=[
            pl.BlockSpec((1, lc, cw), lambda bi, g: (bi, 0, g)),
            pl.BlockSpec((1, lc, cw), lambda bi, g: (bi, 0, ng + g)),
            pl.BlockSpec((1, lc, cw), lambda bi, g: (bi, 0, 2 * ng + g)),
        ],
        out_specs=pl.BlockSpec((1, lc, cw), lambda bi, g: (bi, 0, g)),
        compiler_params=_cparams(("parallel", "parallel")),
        name="ctx_attn",
    )(pc3, pc3, pc3)


def _mix_kernel(xin_ref, gb_ref, gc_ref, su_ref, sv_ref, xp_ref, cp_ref, xn_ref, cn_ref,
                cw_ref, sgn_ref, sgw_ref, sgb_ref, o_ref, *, tm, cwid, groups, halo):
    i = pl.program_id(1)
    last = pl.num_programs(1) - 1
    z = gc_ref[0].astype(F32) * xin_ref[0].astype(F32)
    zp = cp_ref[0, halo - 1:halo, :].astype(F32) * xp_ref[0, halo - 1:halo, :].astype(F32)
    zn = cn_ref[0, 0:1, :].astype(F32) * xn_ref[0, 0:1, :].astype(F32)
    zp = jnp.where(i == 0, 0.0, zp)
    zn = jnp.where(i == last, 0.0, zn)
    row = lax.broadcasted_iota(jnp.int32, (tm, cwid), 0)
    z_m1 = jnp.where(row == 0, zp, pltpu.roll(z, 1, axis=0))
    z_p1 = jnp.where(row == tm - 1, zn, pltpu.roll(z, tm - 1, axis=0))
    conv = cw_ref[0:1, :] * z_m1 + cw_ref[1:2, :] * z + cw_ref[2:3, :] * z_p1
    o_ref[0, :, 0:cwid] = (gb_ref[0].astype(F32) * conv).astype(BF16)

    sv = sv_ref[0].astype(F32)
    ms = jnp.mean(sv * sv, axis=-1, keepdims=True)
    vn = (sv * lax.rsqrt(ms + EPS) * sgn_ref[...]).astype(BF16)
    for g in range(groups):
        gs = slice(g * LANES, (g + 1) * LANES)
        for c in range(tm // CHUNK):
            ts = slice(c * CHUNK, (c + 1) * CHUNK)
            mixed = _dot(sgw_ref[g], vn[ts, gs]) + sgb_ref[g]
            o_ref[0, ts, cwid + g * LANES:cwid + (g + 1) * LANES] = (
                su_ref[0, ts, gs].astype(F32) * mixed).astype(BF16)


def _mix(p3, conv_w, sg_norm, sg_w_bf, sg_b, *, col0, cwid, swid):
    b, l, _ = p3.shape
    groups = sg_w_bf.shape[0]
    assert swid == groups * LANES and cwid == swid and l % CHUNK == 0 and col0 % cwid == 0
    tm = _tile(l, 512, CHUNK)
    halo = 16
    nt = l // tm
    c0 = col0 // cwid
    hb = tm // halo
    nhb = l // halo

    def main(k):
        return pl.BlockSpec((1, tm, cwid), lambda bi, i: (bi, i, c0 + k))

    def prev(k):
        return pl.BlockSpec((1, halo, cwid), lambda bi, i: (bi, jnp.maximum(i * hb - 1, 0), c0 + k))

    def nxt(k):
        return pl.BlockSpec((1, halo, cwid), lambda bi, i: (bi, jnp.minimum((i + 1) * hb, nhb - 1), c0 + k))

    full = lambda shape: pl.BlockSpec(shape, lambda bi, i: (0,) * len(shape))
    return pl.pallas_call(
        functools.partial(_mix_kernel, tm=tm, cwid=cwid, groups=groups, halo=halo),
        out_shape=jax.ShapeDtypeStruct((b, l, cwid + swid), BF16),
        grid=(b, nt),
        in_specs=[main(0), main(1), main(2), main(3), main(4), prev(0), prev(2), nxt(0), nxt(2),
                  full((3, cwid)), full((1, swid)), full((groups, CHUNK, CHUNK)), full((groups, CHUNK, 1))],
        out_specs=pl.BlockSpec((1, tm, cwid + swid), lambda bi, i: (bi, i, 0)),
        compiler_params=_cparams(("parallel", "parallel")),
        name="conv_gmlp",
    )(p3, p3, p3, p3, p3, p3, p3, p3, p3, conv_w, sg_norm.reshape(1, swid), sg_w_bf,
      sg_b.reshape(groups, CHUNK, 1))


def _outproj_kernel(a_ref, m_ref, wa_ref, wm_ref, x_ref, g_ref, o_ref):
    acc = _dot(a_ref[...], wa_ref[0]) + _dot(m_ref[...], wm_ref[0])
    o_ref[...] = x_ref[...] + g_ref[0] * acc


def _outproj(a2d, m2d, w_bf, layer, x2d, gate, seq_len):
    t, d = x2d.shape
    ka, km = a2d.shape[1], m2d.shape[1]
    assert ka == km
    tm = _tile(seq_len, 1024, 16)
    tn = _tile(d, 1024, LANES)
    nrows = gate.shape[0]
    row = (lambda i: (i * tm) // seq_len) if nrows > 1 else (lambda i: 0)
    return pl.pallas_call(
        _outproj_kernel,
        out_shape=jax.ShapeDtypeStruct((t, d), F32),
        grid=(t // tm, d // tn),
        in_specs=[
            pl.BlockSpec((tm, ka), lambda i, j: (i, 0)),
            pl.BlockSpec((tm, km), lambda i, j: (i, 0)),
            pl.BlockSpec((1, ka, tn), lambda i, j: (layer, 0, j)),
            pl.BlockSpec((1, km, tn), lambda i, j: (layer, 1, j)),
            pl.BlockSpec((tm, tn), lambda i, j: (i, j)),
            pl.BlockSpec((1, 1, tn), lambda i, j: (row(i), 0, j)),
        ],
        out_specs=pl.BlockSpec((tm, tn), lambda i, j: (i, j)),
        compiler_params=_cparams(("parallel", "parallel")),
        name="outproj",
    )(a2d, m2d, w_bf, w_bf, x2d, gate)


def _norm_router_kernel(x_ref, sh_ref, sc_ref, g_ref, wh_ref, wl_ref, h_ref, aff_ref, hhi_scr, hlo_scr):
    tm, d = x_ref.shape
    half = d // 2
    n_exp = wh_ref.shape[0]
    rc = LANES if tm % LANES == 0 else tm
    cb = NORM_COLS if half % NORM_COLS == 0 else half
    w_both = jnp.concatenate([wh_ref[...], wl_ref[...]], axis=0)

    logit_chunks = []
    for r0 in range(0, tm, rc):
        rows = slice(r0, r0 + rc)
        xf = x_ref[rows, :]
        rs = lax.rsqrt(jnp.mean(xf * xf, axis=-1, keepdims=True) + EPS)
        for c0 in range(0, half, cb):
            bits = []
            for cols in (slice(c0, c0 + cb), slice(half + c0, half + c0 + cb)):
                gmod = g_ref[:, cols] * (1.0 + sc_ref[0, :, cols])
                h = x_ref[rows, cols] * rs * gmod + sh_ref[0, :, cols]
                h_hi = h.astype(BF16)
                h_hi32 = h_hi.astype(F32)
                hhi_scr[rows, cols] = h_hi
                hlo_scr[rows, cols] = (h - h_hi32).astype(BF16)
                bits.append(pltpu.bitcast(h_hi32, jnp.int32))
            h_ref[rows, c0:c0 + cb] = (bits[1] & jnp.int32(-65536)) | lax.shift_right_logical(bits[0], 16)
        both = _dot_nt(w_both, hhi_scr[rows, :])
        logit_chunks.append(both[:n_exp] + both[n_exp:] + _dot_nt(wh_ref[...], hlo_scr[rows, :]))
    logits = jnp.concatenate(logit_chunks, axis=1)
    m = jnp.max(logits, axis=0, keepdims=True)
    e = jnp.exp(logits - m)
    aff_ref[0] = e / jnp.sum(e, axis=0, keepdims=True)


def _norm_router(x2d, shift, scale, gain, wr_hi, wr_lo, seq_len):
    t, d = x2d.shape
    e = wr_hi.shape[0]
    nseq = t // seq_len
    tm = _tile(seq_len, 512, LANES)
    per = seq_len // tm
    nrows = shift.shape[0]
    row = (lambda i: i // per) if nrows > 1 else (lambda i: 0)
    return pl.pallas_call(
        _norm_router_kernel,
        out_shape=(jax.ShapeDtypeStruct((t, d // 2), jnp.int32), jax.ShapeDtypeStruct((nseq, e, seq_len), F32)),
        grid=(t // tm,),
        in_specs=[
            pl.BlockSpec((tm, d), lambda i: (i, 0)),
            pl.BlockSpec((1, 1, d), lambda i: (row(i), 0, 0)),
            pl.BlockSpec((1, 1, d), lambda i: (row(i), 0, 0)),
            pl.BlockSpec((1, d), lambda i: (0, 0)),
            pl.BlockSpec((e, d), lambda i: (0, 0)),
            pl.BlockSpec((e, d), lambda i: (0, 0)),
        ],
        out_specs=(pl.BlockSpec((tm, d // 2), lambda i: (i, 0)),
                   pl.BlockSpec((1, e, tm), lambda i: (i // per, 0, i % per))),
        scratch_shapes=[pltpu.VMEM((tm, d), BF16), pltpu.VMEM((tm, d), BF16)],
        compiler_params=_cparams(("parallel",)),
        name="norm_router",
    )(x2d, shift, scale, gain.reshape(1, d), wr_hi, wr_lo)


def _topk_kernel(aff_ref, idx_ref, gate_ref, thr_scr, *, nr, cap, capp, n_exp):
    bits_all = pltpu.bitcast(aff_ref[0], jnp.int32)

    def count_all(mask):
        c = jnp.sum(mask.astype(F32), axis=1, keepdims=True)
        return jnp.sum(c, axis=2, keepdims=True)

    def search(i, thr):
        cand = thr | jnp.left_shift(jnp.int32(1), 30 - i)
        return jnp.where(count_all(bits_all >= cand) >= cap, cand, thr)

    thr_scr[...] = lax.fori_loop(0, 31, search, jnp.zeros((n_exp, 1, 1), jnp.int32))

    def per_expert(e, carry):
        _topk_compact(aff_ref[0, e], thr_scr[e], idx_ref.at[0, e], gate_ref.at[0, e], nr=nr, cap=cap, capp=capp)
        return carry

    lax.fori_loop(0, n_exp, per_expert, 0)


def _topk_compact(aff, thr, idx_ref, gate_ref, *, nr, cap, capp):
    bits = pltpu.bitcast(aff, jnp.int32)

    def count(mask):
        c = jnp.sum(mask.astype(F32), axis=0, keepdims=True)
        return jnp.sum(c, axis=1, keepdims=True)

    gt = bits > thr
    eq = bits == thr
    need = cap - count(gt)

    lane_u = lax.broadcasted_iota(jnp.int32, (LANES, LANES), 0)
    lane_t = lax.broadcasted_iota(jnp.int32, (LANES, LANES), 1)
    tri_incl = (lane_u <= lane_t).astype(BF16)
    row_a = lax.broadcasted_iota(jnp.int32, (nr, nr), 0)
    row_b = lax.broadcasted_iota(jnp.int32, (nr, nr), 1)
    low_strict = (row_b < row_a).astype(BF16)

    def row_offsets(maskf):
        tot = jnp.sum(maskf, axis=1, keepdims=True)
        totb = jnp.broadcast_to(tot, (nr, LANES)).astype(BF16)
        return tot, _dot(low_strict, totb)[:, 0:1]

    eqf = eq.astype(F32)
    _, eq_off = row_offsets(eqf)
    eq_rank = _dot(eq.astype(BF16), tri_incl) - eqf + eq_off
    sel = gt | (eq & (eq_rank < need))
    self32 = sel.astype(F32)
    selb = sel.astype(BF16)

    tot, offx = row_offsets(self32)
    offi = offx + tot
    slot = lax.broadcasted_iota(jnp.int32, (1, capp), 1).astype(F32)
    r_of = jnp.sum((offi <= slot).astype(F32), axis=0, keepdims=True)
    rows_col = lax.broadcasted_iota(jnp.int32, (nr, capp), 0).astype(F32)
    onehot = rows_col == r_of
    onehot_b = onehot.astype(BF16)
    local = slot - jnp.sum(jnp.where(onehot, offx, 0.0), axis=0, keepdims=True)
    incl_t = _dot_nt((lane_t <= lane_u).astype(BF16), selb)
    pref = _dot(incl_t.astype(BF16), onehot_b)
    lane_of = jnp.sum((pref <= local).astype(F32), axis=0, keepdims=True)
    idx = r_of * LANES + lane_of
    idx_ref[...] = jnp.clip(idx, 0, nr * LANES - 1).astype(jnp.int32)

    eye = (lane_u == lane_t).astype(BF16)
    a_hi = aff.astype(BF16)
    r1 = aff - a_hi.astype(F32)
    a_mid = r1.astype(BF16)
    a_lo = (r1 - a_mid.astype(F32)).astype(BF16)
    rows_t = jnp.zeros((LANES, capp), F32)
    for part in (a_hi, a_mid, a_lo):
        part_t = _dot_nt(eye, part).astype(BF16)
        rows_t = rows_t + _dot(part_t, onehot_b)
    lane_col = lax.broadcasted_iota(jnp.int32, (LANES, capp), 0).astype(F32)
    gate_ref[...] = jnp.sum(jnp.where(lane_col == lane_of, rows_t, 0.0), axis=0, keepdims=True)


def _topk(aff, cap):
    s, e, l = aff.shape
    nr = l // LANES
    assert l % (LANES * SUBLANES) == 0
    capp = -(-cap // LANES) * LANES
    shp = (s, e, 1, capp)
    spec = pl.BlockSpec((1, e, 1, capp), lambda si: (si, 0, 0, 0))
    return pl.pallas_call(
        functools.partial(_topk_kernel, nr=nr, cap=cap, capp=capp, n_exp=e),
        out_shape=(jax.ShapeDtypeStruct(shp, jnp.int32), jax.ShapeDtypeStruct(shp, F32)),
        grid=(s,),
        in_specs=[pl.BlockSpec((1, e, nr, LANES), lambda si: (si, 0, 0, 0))],
        out_specs=(spec, spec),
        scratch_shapes=[pltpu.VMEM((e, 1, 1), jnp.int32)],
        compiler_params=_cparams(("parallel",)),
        name="expert_topk",
    )(aff.reshape(s, e, nr, LANES))


def _ffn_kernel(*refs, n_src, segs, n_exp, m_rows, lat_rows, per_step, rb, n_up, n_dn, fu, half):
    idx_refs = refs[:n_src]
    h_hbms = refs[n_src:2 * n_src]
    gate_ref, wg_ref, wu_ref, wd_ref, o_ref, hsp, act, sem = refs[2 * n_src:]
    e = pl.program_id(0)
    st = pl.program_id(1)
    hi_mask = jnp.int32(-65536)

    def issue_rows(ee, lo, hi):
        for row0, nrows, si, s, seq_len, capp in segs:
            a, b = max(lo, row0), min(hi, row0 + nrows)
            if a >= b:
                continue

            def body(r, c, row0=row0, si=si, s=s, seq_len=seq_len, capp=capp):
                tok = idx_refs[si][(s * n_exp + ee) * capp + (r - row0)]
                pltpu.make_async_copy(h_hbms[si].at[pl.ds(s * seq_len + tok, 1), :],
                                      hsp.at[pl.ds(r, 1), :], sem).start()
                return c

            lax.fori_loop(a, b, body, 0, unroll=8)

    def wait_rows():
        pltpu.make_async_copy(h_hbms[0].at[pl.ds(0, m_rows), :], hsp, sem).wait()

    @pl.when((e == 0) & (st == 0))
    def _():
        issue_rows(0, 0, m_rows)

    @pl.when(st == 0)
    def _():
        wait_rows()

    @pl.when(st < n_up)
    def _():
        w = jnp.concatenate([wg_ref[0, 0].astype(BF16), wu_ref[0, 0].astype(BF16)], axis=1)
        for rbi in range(m_rows // rb):
            rows = slice(rbi * rb, (rbi + 1) * rb)
            hp = hsp[rows, :]
            lo = pltpu.bitcast(hp << 16, F32).astype(BF16)
            hi = pltpu.bitcast(hp & hi_mask, F32).astype(BF16)
            gu = _dot(lo, w[:half]) + _dot(hi, w[half:])
            a = (_silu(gu[:, :fu]) * gu[:, fu:]).astype(BF16)
            for k in range(n_up):
                @pl.when(st == k)
                def _(k=k, a=a, rows=rows):
                    act[rows, k * fu:(k + 1) * fu] = a

    @pl.when(st >= n_up)
    def _():
        k = st - n_up
        ee = jnp.minimum(e + 1, n_exp - 1)
        _, cap0, _, _, seq_len0, capp0 = segs[0]
        per_seq = cap0 // per_step
        s0 = k // per_seq
        base_idx = (s0 * n_exp + ee) * capp0 + (k % per_seq) * per_step
        base_src = s0 * seq_len0
        base_dst = k * per_step

        @pl.when(k == 0)
        def _():
            issue_rows(ee, lat_rows, m_rows)

        wd = wd_ref[0, 0].astype(BF16)
        nblk = m_rows // rb
        for rbi in range(nblk):
            rows = slice(rbi * rb, (rbi + 1) * rb)
            y = _dot(act[rows, :], wd) * gate_ref[0, rows, :]
            o_ref[0, rows, :] = y.astype(BF16)
            for r in range((per_step * rbi) // nblk, (per_step * (rbi + 1)) // nblk):
                tok = idx_refs[0][base_idx + r]
                pltpu.make_async_copy(h_hbms[0].at[pl.ds(base_src + tok, 1), :],
                                      hsp.at[pl.ds(base_dst + r, 1), :], sem).start()

        @pl.when((e == n_exp - 1) & (k == n_dn - 1))
        def _():
            wait_rows()


def _expert_ffn(routed, w_gate, w_up, w_down, layer):
    _, n_exp, d, ff = w_gate.shape
    n_src = len(routed)
    segs, gates, row0 = [], [], 0
    for si, (hp, idx, gate, seq_len, cap) in enumerate(routed):
        n_seq, _, _, capp = idx.shape
        for s in range(n_seq):
            segs.append((row0, cap, si, s, seq_len, capp))
            row0 += cap
        gates.append(jnp.transpose(gate[:, :, 0, :cap], (1, 0, 2)).reshape(n_exp, n_seq * cap))
    m_rows = row0
    gate_all = jnp.concatenate(gates, axis=1).reshape(n_exp, m_rows, 1)
    rb = _tile(m_rows, -(-m_rows // 4), 16) if m_rows % 64 == 0 else m_rows
    fu = _tile(ff, 2 * LANES, LANES)
    dnw = _tile(d, 512, LANES)
    n_up, n_dn = ff // fu, d // dnw
    lat_rows = routed[0][1].shape[0] * routed[0][4]
    per_step = lat_rows // n_dn
    assert routed[0][0].shape[0] >= m_rows
    assert lat_rows % n_dn == 0 and routed[0][4] % per_step == 0

    def up_idx(ei, st, *_):
        return layer, ei, 0, jnp.minimum(st, n_up - 1)

    def dn_idx(ei, st, *_):
        return ei, 0, jnp.clip(st - n_up, 0, n_dn - 1)

    grid_spec = pltpu.PrefetchScalarGridSpec(
        num_scalar_prefetch=n_src,
        grid=(n_exp, n_up + n_dn),
        in_specs=[pl.BlockSpec(memory_space=pl.ANY)] * n_src + [
            pl.BlockSpec((1, m_rows, 1), lambda ei, st, *_: (ei, 0, 0)),
            pl.BlockSpec((1, 1, d, fu), up_idx),
            pl.BlockSpec((1, 1, d, fu), up_idx),
            pl.BlockSpec((1, 1, ff, dnw), lambda ei, st, *_: (layer,) + dn_idx(ei, st)),
        ],
        out_specs=pl.BlockSpec((1, m_rows, dnw), dn_idx),
        scratch_shapes=[pltpu.VMEM((m_rows, d // 2), jnp.int32), pltpu.VMEM((m_rows, ff), BF16),
                        pltpu.SemaphoreType.DMA],
    )
    return pl.pallas_call(
        functools.partial(_ffn_kernel, n_src=n_src, segs=tuple(segs), n_exp=n_exp, m_rows=m_rows,
                          lat_rows=lat_rows, per_step=per_step, rb=rb, n_up=n_up, n_dn=n_dn, fu=fu, half=d // 2),
        out_shape=jax.ShapeDtypeStruct((n_exp, m_rows, d), BF16),
        grid_spec=grid_spec,
        compiler_params=_cparams(("arbitrary", "arbitrary")),
        name="expert_ffn",
    )(*[r[1].reshape(-1) for r in routed], *[r[0] for r in routed], gate_all, w_gate, w_up, w_down)


def _combine_kernel(idx_ref, y_ref, g_ref, x_hbm, o_hbm, xbuf, gsem, ssem, *, seq_len, capp, tr, n_seq, n_exp):
    del x_hbm
    e = pl.program_id(0)
    s = pl.program_id(1)
    j = pl.program_id(2)
    nj = pl.num_programs(2)
    k = (e * n_seq + s) * nj + j
    total = n_exp * n_seq * nj
    slot = k % 2

    def row_pairs(kk, sl):
        jj = kk % nj
        ss = (kk // nj) % n_seq
        ee = kk // (nj * n_seq)
        base = (ss * n_exp + ee) * capp + jj * tr

        def one(r):
            tok = idx_ref[base + r]
            return o_hbm.at[pl.ds(ss * seq_len + tok, 1), :], xbuf.at[sl, pl.ds(r, 1), :]

        return one

    def gather_start(kk, sl):
        one = row_pairs(kk, sl)

        def body(r, c):
            src, dst = one(r)
            pltpu.make_async_copy(src, dst, gsem.at[sl]).start()
            return c

        lax.fori_loop(0, tr, body, 0, unroll=8)

    def scatter_start(kk, sl):
        one = row_pairs(kk, sl)

        def body(r, c):
            dst, src = one(r)
            pltpu.make_async_copy(src, dst, ssem.at[sl]).start()
            return c

        lax.fori_loop(0, tr, body, 0, unroll=8)

    def gather_wait(sl):
        pltpu.make_async_copy(o_hbm.at[pl.ds(0, tr), :], xbuf.at[sl], gsem.at[sl]).wait()

    def scatter_wait(sl):
        pltpu.make_async_copy(xbuf.at[sl], o_hbm.at[pl.ds(0, tr), :], ssem.at[sl]).wait()

    @pl.when(k == 0)
    def _():
        gather_start(k, slot)

    gather_wait(slot)

    @pl.when(k + 1 < total)
    def _():
        @pl.when(k >= 1)
        def _():
            scatter_wait(1 - slot)

        gather_start(k + 1, 1 - slot)

    xbuf[slot] = xbuf[slot] + g_ref[0] * y_ref[0].astype(F32)
    scatter_start(k, slot)

    @pl.when(k == total - 1)
    def _():
        scatter_wait(slot)

        @pl.when(total > 1)
        def _():
            scatter_wait(1 - slot)


def _combine(x2d, y, idx, g2, seq_len, cap, row0):
    t, d = x2d.shape
    s, e, _, capp = idx.shape
    tr = _tile(cap, 512, 16)
    nj = cap // tr
    assert s * nj >= 2, "consecutive steps must touch disjoint rows"
    assert row0 % tr == 0 and t >= tr
    nrows = g2.shape[0]
    grid_spec = pltpu.PrefetchScalarGridSpec(
        num_scalar_prefetch=1,
        grid=(e, s, nj),
        in_specs=[
            pl.BlockSpec((1, tr, d), lambda ei, si, j, idx_r: (ei, row0 // tr + si * nj + j, 0)),
            pl.BlockSpec((1, 1, d), lambda ei, si, j, idx_r: (si if nrows > 1 else 0, 0, 0)),
            pl.BlockSpec(memory_space=pl.ANY),
        ],
        out_specs=pl.BlockSpec(memory_space=pl.ANY),
        scratch_shapes=[pltpu.VMEM((2, tr, d), F32), pltpu.SemaphoreType.DMA((2,)), pltpu.SemaphoreType.DMA((2,))],
    )
    return pl.pallas_call(
        functools.partial(_combine_kernel, seq_len=seq_len, capp=capp, tr=tr, n_seq=s, n_exp=e),
        out_shape=jax.ShapeDtypeStruct((t, d), F32),
        grid_spec=grid_spec,
        input_output_aliases={3: 0},
        compiler_params=_cparams(("arbitrary", "arbitrary", "arbitrary")),
        name="moe_combine",
    )(idx.reshape(-1), y, g2, x2d)


def _moe(srcs, gain, wr_hi, wr_lo, w_gate, w_up, w_down, layer):
    n_exp = wr_hi.shape[0]
    routed = []
    for x2d, shift, scale, _, seq_len in srcs:
        cap = CAPACITY_FACTOR * seq_len // n_exp
        hp, aff = _norm_router(x2d, shift, scale, gain, wr_hi, wr_lo, seq_len)
        lpad = -(-seq_len // (LANES * SUBLANES)) * (LANES * SUBLANES)
        if lpad != seq_len:
            aff = jnp.pad(aff, ((0, 0), (0, 0), (0, lpad - seq_len)), constant_values=-1.0)
        idx, gate = _topk(aff, cap)
        routed.append((hp, idx, gate, seq_len, cap))
    y = _expert_ffn(routed, w_gate, w_up, w_down, layer)
    outs, row0 = [], 0
    for (x2d, _, _, g2, seq_len), (_, idx, _, _, cap) in zip(srcs, routed):
        outs.append(_combine(x2d, y, idx, g2, seq_len, cap, row0))
        row0 += idx.shape[0] * cap
    return outs


def kernel(x, c, ctx, c_ctx, w_ada, b_ada, norm1, norm2, w_in, q_norm, k_norm, rpb, conv_w, sg_norm, sg_w,
           sg_b, w_out, w_router, w_gate, w_up, w_down):
    b, n, d = x.shape
    lc = ctx.shape[1]
    depth = w_ada.shape[0]
    na = d // 2
    cwid = d // 4
    swid = d - na - cwid
    in_cols = w_in.shape[2]
    assert in_cols == 3 * na + 3 * cwid + 2 * swid and b + 1 <= SUBLANES

    cvec = jnp.concatenate([c, c_ctx[None, :], jnp.zeros((SUBLANES - b - 1, d), F32)], axis=0)
    mod = _adaln(cvec, w_ada, b_ada).reshape(depth, SUBLANES, N_MOD, 1, d)

    w_in_bf = _cast_bf16(w_in)
    w_out_bf = _cast_bf16(w_out)
    xl = x.reshape(b * n, d)
    xc = ctx.reshape(b * lc, d)
    for l in range(depth):
        update_ctx = l < depth - 1
        lat = [mod[l, :b, k] for k in range(N_MOD)]
        cx = [mod[l, b:b + 1, k] for k in range(N_MOD)]
        sg_w_bf = sg_w[l].astype(BF16)
        wr_t = w_router[l].T
        wr_hi = wr_t.astype(BF16)
        wr_lo = (wr_t - wr_hi.astype(F32)).astype(BF16)
        bias = _bias_table(rpb[l], n // GRID_W)

        c_col0, c_ncols = (0, in_cols) if update_ctx else (na, 2 * na)
        pc = _inproj(xc, cx[0], cx[1], norm1[l], w_in_bf, l, q_norm[l], k_norm[l], lc,
                     col0=c_col0, ncols=c_ncols, na_width=na).reshape(b, lc, c_ncols)
        p = _inproj(xl, lat[0], lat[1], norm1[l], w_in_bf, l, q_norm[l], k_norm[l], n,
                    col0=0, ncols=in_cols, na_width=na).reshape(b, n, in_cols)

        a = _natten(p, pc, bias, na_width=na, kc_col=na - c_col0, vc_col=2 * na - c_col0)
        m = _mix(p, conv_w[l], sg_norm[l], sg_w_bf, sg_b[l], col0=3 * na, cwid=cwid, swid=swid)
        xl = _outproj(a.reshape(b * n, na), m.reshape(b * n, cwid + swid), w_out_bf, l, xl, lat[2], n)
        srcs = [(xl, lat[3], lat[4], lat[5], n)]
        if update_ctx:
            a_c = _ctx_attn(pc, na_width=na)
            m_c = _mix(pc, conv_w[l], sg_norm[l], sg_w_bf, sg_b[l], col0=3 * na, cwid=cwid, swid=swid)
            xc = _outproj(a_c.reshape(b * lc, na), m_c.reshape(b * lc, cwid + swid), w_out_bf, l, xc, cx[2], lc)
            srcs.append((xc, cx[3], cx[4], cx[5], lc))
        outs = _moe(srcs, norm2[l], wr_hi, wr_lo, w_gate, w_up, w_down, l)
        xl = outs[0]
        if update_ctx:
            xc = outs[1]
    return xl.reshape(b, n, d)
```

```python
import functools

import jax
import jax.numpy as jnp
from jax import lax
from jax.experimental import pallas as pl
from jax.experimental.pallas import tpu as pltpu

F32 = jnp.float32
BF16 = jnp.bfloat16

GRID_W = 64
HEAD_DIM = 128
NA_KH = 8
NA_KW = 16
CHUNK = 128
N_MOD = 6
CAPACITY_FACTOR = 2
EPS = 1e-6
NEG = -1e30

LANES = 128
SUBLANES = 8
VMEM_LIMIT = 56 * 1024 * 1024

ATT_ROWS = 4
ATT_KROWS = ATT_ROWS + NA_KH
ATT_HPG = 4
ATT_SUB = 2
NORM_ROWS = 64
NORM_COLS = 512
LOG2E = 1.4426950408889634


def _cparams(sem, vmem=VMEM_LIMIT):
    return pltpu.CompilerParams(dimension_semantics=sem, vmem_limit_bytes=vmem)


def _tile(n, pref, align):
    if n <= pref:
        return n
    t = (pref // align) * align
    while t >= align:
        if n % t == 0:
            return t
        t -= align
    raise ValueError(f"no tile for {n} (pref {pref}, align {align})")


def _dot(a, b):
    return jnp.dot(a, b, preferred_element_type=F32)


def _dot_nt(a, b):
    return lax.dot_general(a, b, (((1,), (1,)), ((), ())), preferred_element_type=F32)


def _silu(v):
    return v * jax.nn.sigmoid(v)


def _adaln_kernel(c_ref, w_ref, b_ref, o_ref):
    c = c_ref[...]
    s = _silu(c).astype(BF16)
    o_ref[0] = _dot(s, w_ref[0].astype(BF16)) + b_ref[0]


def _adaln(cvec, w_ada, b_ada):
    depth, d, cols = w_ada.shape
    rows = cvec.shape[0]
    tn = _tile(cols, 512, LANES)
    return pl.pallas_call(
        _adaln_kernel,
        out_shape=jax.ShapeDtypeStruct((depth, rows, cols), F32),
        grid=(depth, cols // tn),
        in_specs=[
            pl.BlockSpec((rows, d), lambda l, j: (0, 0)),
            pl.BlockSpec((1, d, tn), lambda l, j: (l, 0, j)),
            pl.BlockSpec((1, 1, tn), lambda l, j: (l, 0, j)),
        ],
        out_specs=pl.BlockSpec((1, rows, tn), lambda l, j: (l, 0, j)),
        compiler_params=_cparams(("parallel", "parallel")),
        name="adaln",
    )(cvec, w_ada, b_ada.reshape(depth, 1, cols))


def _cast_kernel(x_ref, o_ref):
    o_ref[...] = x_ref[...].astype(BF16)


def _cast_bf16(w):
    depth, r, c = w.shape
    tr = _tile(r, 512, 16)
    tc = _tile(c, 4096, LANES)
    return pl.pallas_call(
        _cast_kernel,
        out_shape=jax.ShapeDtypeStruct(w.shape, BF16),
        grid=(depth, r // tr, c // tc),
        in_specs=[pl.BlockSpec((1, tr, tc), lambda l, i, j: (l, i, j))],
        out_specs=pl.BlockSpec((1, tr, tc), lambda l, i, j: (l, i, j)),
        compiler_params=_cparams(("parallel", "parallel", "parallel")),
        name="cast_bf16",
    )(w)


def _att_cases(rows):
    nrb = rows // ATT_ROWS
    cases = []
    for rb in (0, min(1, nrb - 1), nrb - 1):
        r0 = rb * ATT_ROWS
        u = min(max(r0 - NA_KH // 2, 0), rows - ATT_KROWS)
        per_row = []
        for a in range(ATT_ROWS):
            r = r0 + a
            s = min(max(r - NA_KH // 2, 0), rows - NA_KH)
            per_row.append((s - u, s - r + NA_KH - 1))
        cases.append(per_row)
    return cases


def _bias_kernel(rpb_ref, o_ref, *, cases, n_dr, n_dc):
    h = pl.program_id(0)
    qc = lax.broadcasted_iota(jnp.int32, (GRID_W, GRID_W), 0)
    kc = lax.broadcasted_iota(jnp.int32, (GRID_W, GRID_W), 1)
    cs = jnp.clip(qc - NA_KW // 2, 0, GRID_W - NA_KW)
    col_ok = (kc >= cs) & (kc < cs + NA_KW)
    dc = kc - qc + NA_KW - 1
    neg = jnp.full((GRID_W, GRID_W), NEG, F32)

    def toeplitz(dr):
        t = neg
        for i in range(n_dc):
            t = jnp.where(dc == i, rpb_ref[h, dr, i] * LOG2E, t)
        return jnp.where(col_ok, t, neg)

    tiles = [toeplitz(dr) for dr in range(n_dr)]
    for ci, per_row in enumerate(cases):
        for a, (j_first, dr_first) in enumerate(per_row):
            for j in range(ATT_KROWS):
                inside = j_first <= j < j_first + NA_KH
                t = tiles[dr_first + (j - j_first)] if inside else neg
                o_ref[ci, 0, a * GRID_W:(a + 1) * GRID_W, j * GRID_W:(j + 1) * GRID_W] = t


def _bias_table(rpb_l, rows):
    h, n_dr, n_dc = rpb_l.shape
    cases = _att_cases(rows)
    rq, rk = ATT_ROWS * GRID_W, ATT_KROWS * GRID_W
    return pl.pallas_call(
        functools.partial(_bias_kernel, cases=cases, n_dr=n_dr, n_dc=n_dc),
        out_shape=jax.ShapeDtypeStruct((len(cases), h, rq, rk), F32),
        grid=(h,),
        in_specs=[pl.BlockSpec(memory_space=pltpu.SMEM)],
        out_specs=pl.BlockSpec((len(cases), 1, rq, rk), lambda hh: (0, hh, 0, 0)),
        compiler_params=_cparams(("parallel",)),
        name="rpb_bias",
    )(rpb_l)


def _inproj_kernel(x_hbm, sh_ref, sc_ref, g_ref, w_ref, qg_ref, kg_ref, o_ref, xbuf, h_scr, rs_scr, sem,
                   *, j0, nq, tn, tm):
    i = pl.program_id(0)
    j = pl.program_id(1)

    def x_copy(tile):
        return pltpu.make_async_copy(x_hbm.at[pl.ds(tile * tm, tm), :], xbuf, sem)

    @pl.when((i == 0) & (j == 0))
    def _():
        x_copy(0).start()

    @pl.when(j == 0)
    def _():
        x_copy(i).wait()
        rc = NORM_ROWS if tm % NORM_ROWS == 0 else tm

        def row_scale(c, carry):
            rows = pl.ds(pl.multiple_of(c * rc, rc), rc)
            xf = xbuf[rows, :]
            ms = jnp.mean(xf * xf, axis=-1, keepdims=True)
            rs_scr[rows, :] = jnp.broadcast_to(lax.rsqrt(ms + EPS), (rc, LANES))
            return carry

        def modulate(c, carry):
            rows = pl.ds(pl.multiple_of(c * rc, rc), rc)
            rs = rs_scr[rows, :][:, 0:1]
            cb = NORM_COLS if xbuf.shape[1] % NORM_COLS == 0 else xbuf.shape[1]
            for c0 in range(0, xbuf.shape[1], cb):
                cols = slice(c0, c0 + cb)
                gmod = g_ref[:, cols] * (1.0 + sc_ref[0, :, cols])
                h_scr[rows, cols] = (xbuf[rows, cols] * rs * gmod + sh_ref[0, :, cols]).astype(BF16)
            return carry

        lax.fori_loop(0, tm // rc, row_scale, 0, unroll=2)
        lax.fori_loop(0, tm // rc, modulate, 0, unroll=2)

    @pl.when((j == 1) & (i + 1 < pl.num_programs(0)))
    def _():
        x_copy(i + 1).start()

    acc = _dot(h_scr[...], w_ref[0])
    jj = j + j0

    def head_norm(gain):
        for hh in range(tn // HEAD_DIM):
            sl = slice(hh * HEAD_DIM, (hh + 1) * HEAD_DIM)
            blk = acc[:, sl]
            ms = jnp.mean(blk * blk, axis=-1, keepdims=True)
            o_ref[:, sl] = (blk * lax.rsqrt(ms + EPS) * gain).astype(BF16)

    @pl.when(jj < nq)
    def _():
        head_norm(qg_ref[...] * (HEAD_DIM ** -0.5 * LOG2E))

    @pl.when((jj >= nq) & (jj < 2 * nq))
    def _():
        head_norm(kg_ref[...])

    @pl.when(jj >= 2 * nq)
    def _():
        o_ref[...] = acc.astype(BF16)


def _inproj(x2d, shift, scale, gain, w_bf, layer, q_gain, k_gain, seq_len, *, col0, ncols, na_width):
    t, d = x2d.shape
    nrows = shift.shape[0]
    tm = _tile(seq_len if nrows > 1 else t, 1024, 16)
    tn = _tile(na_width // 2, 1024, HEAD_DIM)
    assert col0 % tn == 0 and ncols % tn == 0 and t % tm == 0
    assert ncols // tn >= 2, "the next token tile is prefetched at column step 1"
    row = (lambda i: (i * tm) // seq_len) if nrows > 1 else (lambda i: 0)
    j0 = col0 // tn
    return pl.pallas_call(
        functools.partial(_inproj_kernel, j0=j0, nq=na_width // tn, tn=tn, tm=tm),
        out_shape=jax.ShapeDtypeStruct((t, ncols), BF16),
        grid=(t // tm, ncols // tn),
        in_specs=[
            pl.BlockSpec(memory_space=pl.ANY),
            pl.BlockSpec((1, 1, d), lambda i, j: (row(i), 0, 0)),
            pl.BlockSpec((1, 1, d), lambda i, j: (row(i), 0, 0)),
            pl.BlockSpec((1, d), lambda i, j: (0, 0)),
            pl.BlockSpec((1, d, tn), lambda i, j: (layer, 0, j + j0)),
            pl.BlockSpec((1, HEAD_DIM), lambda i, j: (0, 0)),
            pl.BlockSpec((1, HEAD_DIM), lambda i, j: (0, 0)),
        ],
        out_specs=pl.BlockSpec((tm, tn), lambda i, j: (i, j)),
        scratch_shapes=[pltpu.VMEM((tm, d), F32), pltpu.VMEM((tm, d), BF16), pltpu.VMEM((tm, LANES), F32),
                        pltpu.SemaphoreType.DMA],
        compiler_params=_cparams(("arbitrary", "arbitrary")),
        name="norm_inproj",
    )(x2d, shift, scale, gain.reshape(1, d), w_bf, q_gain.reshape(1, HEAD_DIM), k_gain.reshape(1, HEAD_DIM))


def _natten_kernel(q_ref, k_ref, v_ref, kc_ref, vc_ref, *rest, rows):
    b_refs, o_ref = rest[:ATT_SUB], rest[ATT_SUB]
    nk = ATT_KROWS * GRID_W
    rq = ATT_ROWS * GRID_W
    lc = kc_ref.shape[1]
    ones_w = jnp.ones((nk, HEAD_DIM), BF16)
    ones_c = jnp.ones((lc, HEAD_DIM), BF16)

    def lane_tiles(s):
        return [s[:, j * LANES:(j + 1) * LANES] for j in range(s.shape[1] // LANES)]

    for sb in range(ATT_SUB):
        rb = pl.program_id(2) * ATT_SUB + sb
        u = jnp.clip(rb * ATT_ROWS - NA_KH // 2, 0, rows - ATT_KROWS)
        tok0 = pl.multiple_of(u * GRID_W, GRID_W)
        qrows = slice(sb * rq, (sb + 1) * rq)
        for hh in range(ATT_HPG):
            sl = slice(hh * HEAD_DIM, (hh + 1) * HEAD_DIM)
            q = q_ref[0, qrows, sl]
            kw = k_ref[0, pl.ds(tok0, nk), sl]
            kc = kc_ref[0, :, sl]
            vw = jnp.concatenate([v_ref[0, pl.ds(tok0, nk), sl], ones_w], axis=1)
            vc = jnp.concatenate([vc_ref[0, :, sl], ones_c], axis=1)
            s_loc = _dot_nt(q, kw) + b_refs[sb][0, hh]
            s_ctx = _dot_nt(q, kc)
            m = jnp.max(functools.reduce(jnp.maximum, lane_tiles(s_loc) + lane_tiles(s_ctx)), axis=-1, keepdims=True)
            p_loc = jnp.exp2(s_loc - m).astype(BF16)
            p_ctx = jnp.exp2(s_ctx - m).astype(BF16)
            o = _dot(p_loc, vw) + _dot(p_ctx, vc)
            o_ref[0, qrows, sl] = (o[:, :HEAD_DIM] / o[:, HEAD_DIM:]).astype(BF16)


def _natten(p3, pc3, bias, *, na_width, kc_col, vc_col):
    b, n, _ = p3.shape
    lc = pc3.shape[1]
    rows = n // GRID_W
    assert rows % (ATT_ROWS * ATT_SUB) == 0 and rows >= ATT_KROWS
    nrb = rows // ATT_ROWS
    cw = ATT_HPG * HEAD_DIM
    ng = na_width // cw
    rq = ATT_ROWS * GRID_W

    def bias_spec(sb):
        def idx(bi, g, st):
            rb = st * ATT_SUB + sb
            return jnp.where(rb == 0, 0, jnp.where(rb == nrb - 1, 2, 1)), g, 0, 0
        return pl.BlockSpec((1, ATT_HPG, rq, ATT_KROWS * GRID_W), idx)

    return pl.pallas_call(
        functools.partial(_natten_kernel, rows=rows),
        out_shape=jax.ShapeDtypeStruct((b, n, na_width), BF16),
        grid=(b, ng, nrb // ATT_SUB),
        in_specs=[
            pl.BlockSpec((1, ATT_SUB * rq, cw), lambda bi, g, st: (bi, st, g)),
            pl.BlockSpec((1, n, cw), lambda bi, g, st: (bi, 0, ng + g)),
            pl.BlockSpec((1, n, cw), lambda bi, g, st: (bi, 0, 2 * ng + g)),
            pl.BlockSpec((1, lc, cw), lambda bi, g, st: (bi, 0, kc_col // cw + g)),
            pl.BlockSpec((1, lc, cw), lambda bi, g, st: (bi, 0, vc_col // cw + g)),
        ] + [bias_spec(sb) for sb in range(ATT_SUB)],
        out_specs=pl.BlockSpec((1, ATT_SUB * rq, cw), lambda bi, g, st: (bi, st, g)),
        compiler_params=_cparams(("parallel", "parallel", "arbitrary")),
        name="natten",
    )(p3, p3, p3, pc3, pc3, *([bias] * ATT_SUB))


def _ctx_attn_kernel(q_ref, k_ref, v_ref, o_ref):
    for hh in range(ATT_HPG):
        sl = slice(hh * HEAD_DIM, (hh + 1) * HEAD_DIM)
        s = _dot_nt(q_ref[0, :, sl], k_ref[0, :, sl])
        m = jnp.max(s, axis=-1, keepdims=True)
        p = jnp.exp2(s - m)
        den = jnp.sum(p, axis=-1, keepdims=True)
        o_ref[0, :, sl] = (_dot(p.astype(BF16), v_ref[0, :, sl]) / den).astype(BF16)


def _ctx_attn(pc3, *, na_width):
    b, lc, _ = pc3.shape
    cw = ATT_HPG * HEAD_DIM
    ng = na_width // cw
    return pl.pallas_call(
        _ctx_attn_kernel,
        out_shape=jax.ShapeDtypeStruct((b, lc, na_width), BF16),
        grid=(b, ng),
        in_specs=[
            pl.BlockSpec((1, lc, cw), lambda bi, g: (bi, 0, g)),
            pl.BlockSpec((1, lc, cw), lambda bi, g: (bi, 0, ng + g)),
            pl.BlockSpec((1, lc, cw), lambda bi, g: (bi, 0, 2 * ng + g)),
        ],
        out_specs=pl.BlockSpec((1, lc, cw), lambda bi, g: (bi, 0, g)),
        compiler_params=_cparams(("parallel", "parallel")),
        name="ctx_attn",
    )(pc3, pc3, pc3)


def _mix_kernel(xin_ref, gb_ref, gc_ref, su_ref, sv_ref, xp_ref, cp_ref, xn_ref, cn_ref,
                cw_ref, sgn_ref, sgw_ref, sgb_ref, o_ref, *, tm, cwid, groups, halo):
    i = pl.program_id(1)
    last = pl.num_programs(1) - 1
    z = gc_ref[0].astype(F32) * xin_ref[0].astype(F32)
    zp = cp_ref[0, halo - 1:halo, :].astype(F32) * xp_ref[0, halo - 1:halo, :].astype(F32)
    zn = cn_ref[0, 0:1, :].astype(F32) * xn_ref[0, 0:1, :].astype(F32)
    zp = jnp.where(i == 0, 0.0, zp)
    zn = jnp.where(i == last, 0.0, zn)
    row = lax.broadcasted_iota(jnp.int32, (tm, cwid), 0)
    z_m1 = jnp.where(row == 0, zp, pltpu.roll(z, 1, axis=0))
    z_p1 = jnp.where(row == tm - 1, zn, pltpu.roll(z, tm - 1, axis=0))
    conv = cw_ref[0:1, :] * z_m1 + cw_ref[1:2, :] * z + cw_ref[2:3, :] * z_p1
    o_ref[0, :, 0:cwid] = (gb_ref[0].astype(F32) * conv).astype(BF16)

    sv = sv_ref[0].astype(F32)
    ms = jnp.mean(sv * sv, axis=-1, keepdims=True)
    vn = (sv * lax.rsqrt(ms + EPS) * sgn_ref[...]).astype(BF16)
    for g in range(groups):
        gs = slice(g * LANES, (g + 1) * LANES)
        for c in range(tm // CHUNK):
            ts = slice(c * CHUNK, (c + 1) * CHUNK)
            mixed = _dot(sgw_ref[g], vn[ts, gs]) + sgb_ref[g]
            o_ref[0, ts, cwid + g * LANES:cwid + (g + 1) * LANES] = (
                su_ref[0, ts, gs].astype(F32) * mixed).astype(BF16)


def _mix(p3, conv_w, sg_norm, sg_w_bf, sg_b, *, col0, cwid, swid):
    b, l, _ = p3.shape
    groups = sg_w_bf.shape[0]
    assert swid == groups * LANES and cwid == swid and l % CHUNK == 0 and col0 % cwid == 0
    tm = _tile(l, 512, CHUNK)
    halo = 16
    nt = l // tm
    c0 = col0 // cwid
    hb = tm // halo
    nhb = l // halo

    def main(k):
        return pl.BlockSpec((1, tm, cwid), lambda bi, i: (bi, i, c0 + k))

    def prev(k):
        return pl.BlockSpec((1, halo, cwid), lambda bi, i: (bi, jnp.maximum(i * hb - 1, 0), c0 + k))

    def nxt(k):
        return pl.BlockSpec((1, halo, cwid), lambda bi, i: (bi, jnp.minimum((i + 1) * hb, nhb - 1), c0 + k))

    full = lambda shape: pl.BlockSpec(shape, lambda bi, i: (0,) * len(shape))
    return pl.pallas_call(
        functools.partial(_mix_kernel, tm=tm, cwid=cwid, groups=groups, halo=halo),
        out_shape=jax.ShapeDtypeStruct((b, l, cwid + swid), BF16),
        grid=(b, nt),
        in_specs=[main(0), main(1), main(2), main(3), main(4), prev(0), prev(2), nxt(0), nxt(2),
                  full((3, cwid)), full((1, swid)), full((groups, CHUNK, CHUNK)), full((groups, CHUNK, 1))],
        out_specs=pl.BlockSpec((1, tm, cwid + swid), lambda bi, i: (bi, i, 0)),
        compiler_params=_cparams(("parallel", "parallel")),
        name="conv_gmlp",
    )(p3, p3, p3, p3, p3, p3, p3, p3, p3, conv_w, sg_norm.reshape(1, swid), sg_w_bf,
      sg_b.reshape(groups, CHUNK, 1))


def _outproj_kernel(a_ref, m_ref, wa_ref, wm_ref, x_ref, g_ref, o_ref):
    acc = _dot(a_ref[...], wa_ref[0]) + _dot(m_ref[...], wm_ref[0])
    o_ref[...] = x_ref[...] + g_ref[0] * acc


def _outproj(a2d, m2d, w_bf, layer, x2d, gate, seq_len):
    t, d = x2d.shape
    ka, km = a2d.shape[1], m2d.shape[1]
    assert ka == km
    tm = _tile(seq_len, 1024, 16)
    tn = _tile(d, 1024, LANES)
    nrows = gate.shape[0]
    row = (lambda i: (i * tm) // seq_len) if nrows > 1 else (lambda i: 0)
    return pl.pallas_call(
        _outproj_kernel,
        out_shape=jax.ShapeDtypeStruct((t, d), F32),
        grid=(t // tm, d // tn),
        in_specs=[
            pl.BlockSpec((tm, ka), lambda i, j: (i, 0)),
            pl.BlockSpec((tm, km), lambda i, j: (i, 0)),
            pl.BlockSpec((1, ka, tn), lambda i, j: (layer, 0, j)),
            pl.BlockSpec((1, km, tn), lambda i, j: (layer, 1, j)),
            pl.BlockSpec((tm, tn), lambda i, j: (i, j)),
            pl.BlockSpec((1, 1, tn), lambda i, j: (row(i), 0, j)),
        ],
        out_specs=pl.BlockSpec((tm, tn), lambda i, j: (i, j)),
        compiler_params=_cparams(("parallel", "parallel")),
        name="outproj",
    )(a2d, m2d, w_bf, w_bf, x2d, gate)


def _norm_router_kernel(x_ref, sh_ref, sc_ref, g_ref, wh_ref, wl_ref, h_ref, aff_ref, hhi_scr, hlo_scr):
    tm, d = x_ref.shape
    half = d // 2
    n_exp = wh_ref.shape[0]
    rc = LANES if tm % LANES == 0 else tm
    cb = NORM_COLS if half % NORM_COLS == 0 else half
    w_both = jnp.concatenate([wh_ref[...], wl_ref[...]], axis=0)

    logit_chunks = []
    for r0 in range(0, tm, rc):
        rows = slice(r0, r0 + rc)
        xf = x_ref[rows, :]
        rs = lax.rsqrt(jnp.mean(xf * xf, axis=-1, keepdims=True) + EPS)
        for c0 in range(0, half, cb):
            bits = []
            for cols in (slice(c0, c0 + cb), slice(half + c0, half + c0 + cb)):
                gmod = g_ref[:, cols] * (1.0 + sc_ref[0, :, cols])
                h = x_ref[rows, cols] * rs * gmod + sh_ref[0, :, cols]
                h_hi = h.astype(BF16)
                h_hi32 = h_hi.astype(F32)
                hhi_scr[rows, cols] = h_hi
                hlo_scr[rows, cols] = (h - h_hi32).astype(BF16)
                bits.append(pltpu.bitcast(h_hi32, jnp.int32))
            h_ref[rows, c0:c0 + cb] = (bits[1] & jnp.int32(-65536)) | lax.shift_right_logical(bits[0], 16)
        both = _dot_nt(w_both, hhi_scr[rows, :])
        logit_chunks.append(both[:n_exp] + both[n_exp:] + _dot_nt(wh_ref[...], hlo_scr[rows, :]))
    logits = jnp.concatenate(logit_chunks, axis=1)
    m = jnp.max(logits, axis=0, keepdims=True)
    e = jnp.exp(logits - m)
    aff_ref[0] = e / jnp.sum(e, axis=0, keepdims=True)


def _norm_router(x2d, shift, scale, gain, wr_hi, wr_lo, seq_len):
    t, d = x2d.shape
    e = wr_hi.shape[0]
    nseq = t // seq_len
    tm = _tile(seq_len, 512, LANES)
    per = seq_len // tm
    nrows = shift.shape[0]
    row = (lambda i: i // per) if nrows > 1 else (lambda i: 0)
    return pl.pallas_call(
        _norm_router_kernel,
        out_shape=(jax.ShapeDtypeStruct((t, d // 2), jnp.int32), jax.ShapeDtypeStruct((nseq, e, seq_len), F32)),
        grid=(t // tm,),
        in_specs=[
            pl.BlockSpec((tm, d), lambda i: (i, 0)),
            pl.BlockSpec((1, 1, d), lambda i: (row(i), 0, 0)),
            pl.BlockSpec((1, 1, d), lambda i: (row(i), 0, 0)),
            pl.BlockSpec((1, d), lambda i: (0, 0)),
            pl.BlockSpec((e, d), lambda i: (0, 0)),
            pl.BlockSpec((e, d), lambda i: (0, 0)),
        ],
        out_specs=(pl.BlockSpec((tm, d // 2), lambda i: (i, 0)),
                   pl.BlockSpec((1, e, tm), lambda i: (i // per, 0, i % per))),
        scratch_shapes=[pltpu.VMEM((tm, d), BF16), pltpu.VMEM((tm, d), BF16)],
        compiler_params=_cparams(("parallel",)),
        name="norm_router",
    )(x2d, shift, scale, gain.reshape(1, d), wr_hi, wr_lo)


def _topk_kernel(aff_ref, idx_ref, gate_ref, thr_scr, *, nr, cap, capp, n_exp):
    bits_all = pltpu.bitcast(aff_ref[0], jnp.int32)

    def count_all(mask):
        c = jnp.sum(mask.astype(F32), axis=1, keepdims=True)
        return jnp.sum(c, axis=2, keepdims=True)

    def search(i, thr):
        cand = thr | jnp.left_shift(jnp.int32(1), 30 - i)
        return jnp.where(count_all(bits_all >= cand) >= cap, cand, thr)

    thr_scr[...] = lax.fori_loop(0, 31, search, jnp.zeros((n_exp, 1, 1), jnp.int32))

    def per_expert(e, carry):
        _topk_compact(aff_ref[0, e], thr_scr[e], idx_ref.at[0, e], gate_ref.at[0, e], nr=nr, cap=cap, capp=capp)
        return carry

    lax.fori_loop(0, n_exp, per_expert, 0)


def _topk_compact(aff, thr, idx_ref, gate_ref, *, nr, cap, capp):
    bits = pltpu.bitcast(aff, jnp.int32)

    def count(mask):
        c = jnp.sum(mask.astype(F32), axis=0, keepdims=True)
        return jnp.sum(c, axis=1, keepdims=True)

    gt = bits > thr
    eq = bits == thr
    need = cap - count(gt)

    lane_u = lax.broadcasted_iota(jnp.int32, (LANES, LANES), 0)
    lane_t = lax.broadcasted_iota(jnp.int32, (LANES, LANES), 1)
    tri_incl = (lane_u <= lane_t).astype(BF16)
    row_a = lax.broadcasted_iota(jnp.int32, (nr, nr), 0)
    row_b = lax.broadcasted_iota(jnp.int32, (nr, nr), 1)
    low_strict = (row_b < row_a).astype(BF16)

    def row_offsets(maskf):
        tot = jnp.sum(maskf, axis=1, keepdims=True)
        totb = jnp.broadcast_to(tot, (nr, LANES)).astype(BF16)
        return tot, _dot(low_strict, totb)[:, 0:1]

    eqf = eq.astype(F32)
    _, eq_off = row_offsets(eqf)
    eq_rank = _dot(eq.astype(BF16), tri_incl) - eqf + eq_off
    sel = gt | (eq & (eq_rank < need))
    self32 = sel.astype(F32)
    selb = sel.astype(BF16)

    tot, offx = row_offsets(self32)
    offi = offx + tot
    slot = lax.broadcasted_iota(jnp.int32, (1, capp), 1).astype(F32)
    r_of = jnp.sum((offi <= slot).astype(F32), axis=0, keepdims=True)
    rows_col = lax.broadcasted_iota(jnp.int32, (nr, capp), 0).astype(F32)
    onehot = rows_col == r_of
    onehot_b = onehot.astype(BF16)
    local = slot - jnp.sum(jnp.where(onehot, offx, 0.0), axis=0, keepdims=True)
    incl_t = _dot_nt((lane_t <= lane_u).astype(BF16), selb)
    pref = _dot(incl_t.astype(BF16), onehot_b)
    lane_of = jnp.sum((pref <= local).astype(F32), axis=0, keepdims=True)
    idx = r_of * LANES + lane_of
    idx_ref[...] = jnp.clip(idx, 0, nr * LANES - 1).astype(jnp.int32)

    eye = (lane_u == lane_t).astype(BF16)
    a_hi = aff.astype(BF16)
    r1 = aff - a_hi.astype(F32)
    a_mid = r1.astype(BF16)
    a_lo = (r1 - a_mid.astype(F32)).astype(BF16)
    rows_t = jnp.zeros((LANES, capp), F32)
    for part in (a_hi, a_mid, a_lo):
        part_t = _dot_nt(eye, part).astype(BF16)
        rows_t = rows_t + _dot(part_t, onehot_b)
    lane_col = lax.broadcasted_iota(jnp.int32, (LANES, capp), 0).astype(F32)
    gate_ref[...] = jnp.sum(jnp.where(lane_col == lane_of, rows_t, 0.0), axis=0, keepdims=True)


def _topk(aff, cap):
    s, e, l = aff.shape
    nr = l // LANES
    assert l % (LANES * SUBLANES) == 0
    capp = -(-cap // LANES) * LANES
    shp = (s, e, 1, capp)
    spec = pl.BlockSpec((1, e, 1, capp), lambda si: (si, 0, 0, 0))
    return pl.pallas_call(
        functools.partial(_topk_kernel, nr=nr, cap=cap, capp=capp, n_exp=e),
        out_shape=(jax.ShapeDtypeStruct(shp, jnp.int32), jax.ShapeDtypeStruct(shp, F32)),
        grid=(s,),
        in_specs=[pl.BlockSpec((1, e, nr, LANES), lambda si: (si, 0, 0, 0))],
        out_specs=(spec, spec),
        scratch_shapes=[pltpu.VMEM((e, 1, 1), jnp.int32)],
        compiler_params=_cparams(("parallel",)),
        name="expert_topk",
    )(aff.reshape(s, e, nr, LANES))


def _ffn_kernel(*refs, n_src, segs, n_exp, m_rows, lat_rows, per_step, rb, n_up, n_dn, fu, half):
    idx_refs = refs[:n_src]
    h_hbms = refs[n_src:2 * n_src]
    gate_ref, wg_ref, wu_ref, wd_ref, o_ref, hsp, act, sem = refs[2 * n_src:]
    e = pl.program_id(0)
    st = pl.program_id(1)
    hi_mask = jnp.int32(-65536)

    def issue_rows(ee, lo, hi):
        for row0, nrows, si, s, seq_len, capp in segs:
            a, b = max(lo, row0), min(hi, row0 + nrows)
            if a >= b:
                continue

            def body(r, c, row0=row0, si=si, s=s, seq_len=seq_len, capp=capp):
                tok = idx_refs[si][(s * n_exp + ee) * capp + (r - row0)]
                pltpu.make_async_copy(h_hbms[si].at[pl.ds(s * seq_len + tok, 1), :],
                                      hsp.at[pl.ds(r, 1), :], sem).start()
                return c

            lax.fori_loop(a, b, body, 0, unroll=8)

    def wait_rows():
        pltpu.make_async_copy(h_hbms[0].at[pl.ds(0, m_rows), :], hsp, sem).wait()

    @pl.when((e == 0) & (st == 0))
    def _():
        issue_rows(0, 0, m_rows)

    @pl.when(st == 0)
    def _():
        wait_rows()

    @pl.when(st < n_up)
    def _():
        w = jnp.concatenate([wg_ref[0, 0].astype(BF16), wu_ref[0, 0].astype(BF16)], axis=1)
        for rbi in range(m_rows // rb):
            rows = slice(rbi * rb, (rbi + 1) * rb)
            hp = hsp[rows, :]
            lo = pltpu.bitcast(hp << 16, F32).astype(BF16)
            hi = pltpu.bitcast(hp & hi_mask, F32).astype(BF16)
            gu = _dot(lo, w[:half]) + _dot(hi, w[half:])
            a = (_silu(gu[:, :fu]) * gu[:, fu:]).astype(BF16)
            for k in range(n_up):
                @pl.when(st == k)
                def _(k=k, a=a, rows=rows):
                    act[rows, k * fu:(k + 1) * fu] = a

    @pl.when(st >= n_up)
    def _():
        k = st - n_up
        ee = jnp.minimum(e + 1, n_exp - 1)
        _, cap0, _, _, seq_len0, capp0 = segs[0]
        per_seq = cap0 // per_step
        s0 = k // per_seq
        base_idx = (s0 * n_exp + ee) * capp0 + (k % per_seq) * per_step
        base_src = s0 * seq_len0

        @pl.when(k == 0)
        def _():
            issue_rows(ee, lat_rows, m_rows)

        for kk in range(n_dn):
            @pl.when(k == kk)
            def _(kk=kk):
                for r in range(per_step):
                    tok = idx_refs[0][base_idx + r]
                    pltpu.make_async_copy(h_hbms[0].at[pl.ds(base_src + tok, 1), :],
                                          hsp.at[pl.ds(kk * per_step + r, 1), :], sem).start()

        wd = wd_ref[0, 0].astype(BF16)
        for rbi in range(m_rows // rb):
            rows = slice(rbi * rb, (rbi + 1) * rb)
            y = _dot(act[rows, :], wd) * gate_ref[0, rows, :]
            o_ref[0, rows, :] = y.astype(BF16)

        @pl.when((e == n_exp - 1) & (k == n_dn - 1))
        def _():
            wait_rows()


def _expert_ffn(routed, w_gate, w_up, w_down, layer):
    _, n_exp, d, ff = w_gate.shape
    n_src = len(routed)
    segs, gates, row0 = [], [], 0
    for si, (hp, idx, gate, seq_len, cap) in enumerate(routed):
        n_seq, _, _, capp = idx.shape
        for s in range(n_seq):
            segs.append((row0, cap, si, s, seq_len, capp))
            row0 += cap
        gates.append(jnp.transpose(gate[:, :, 0, :cap], (1, 0, 2)).reshape(n_exp, n_seq * cap))
    m_rows = row0
    gate_all = jnp.concatenate(gates, axis=1).reshape(n_exp, m_rows, 1)
    rb = _tile(m_rows, -(-m_rows // 4), 16) if m_rows % 64 == 0 else m_rows
    fu = _tile(ff, 2 * LANES, LANES)
    dnw = _tile(d, 512, LANES)
    n_up, n_dn = ff // fu, d // dnw
    lat_rows = routed[0][1].shape[0] * routed[0][4]
    per_step = lat_rows // n_dn
    assert routed[0][0].shape[0] >= m_rows
    assert lat_rows % n_dn == 0 and routed[0][4] % per_step == 0

    def up_idx(ei, st, *_):
        return layer, ei, 0, jnp.minimum(st, n_up - 1)

    def dn_idx(ei, st, *_):
        return ei, 0, jnp.clip(st - n_up, 0, n_dn - 1)

    grid_spec = pltpu.PrefetchScalarGridSpec(
        num_scalar_prefetch=n_src,
        grid=(n_exp, n_up + n_dn),
        in_specs=[pl.BlockSpec(memory_space=pl.ANY)] * n_src + [
            pl.BlockSpec((1, m_rows, 1), lambda ei, st, *_: (ei, 0, 0)),
            pl.BlockSpec((1, 1, d, fu), up_idx),
            pl.BlockSpec((1, 1, d, fu), up_idx),
            pl.BlockSpec((1, 1, ff, dnw), lambda ei, st, *_: (layer,) + dn_idx(ei, st)),
        ],
        out_specs=pl.BlockSpec((1, m_rows, dnw), dn_idx),
        scratch_shapes=[pltpu.VMEM((m_rows, d // 2), jnp.int32), pltpu.VMEM((m_rows, ff), BF16),
                        pltpu.SemaphoreType.DMA],
    )
    return pl.pallas_call(
        functools.partial(_ffn_kernel, n_src=n_src, segs=tuple(segs), n_exp=n_exp, m_rows=m_rows,
                          lat_rows=lat_rows, per_step=per_step, rb=rb, n_up=n_up, n_dn=n_dn, fu=fu, half=d // 2),
        out_shape=jax.ShapeDtypeStruct((n_exp, m_rows, d), BF16),
        grid_spec=grid_spec,
        compiler_params=_cparams(("arbitrary", "arbitrary")),
        name="expert_ffn",
    )(*[r[1].reshape(-1) for r in routed], *[r[0] for r in routed], gate_all, w_gate, w_up, w_down)


def _combine_kernel(idx_ref, y_ref, g_ref, x_hbm, o_hbm, xbuf, gsem, ssem, *, seq_len, capp, tr, n_seq, n_exp):
    del x_hbm
    e = pl.program_id(0)
    s = pl.program_id(1)
    j = pl.program_id(2)
    nj = pl.num_programs(2)
    k = (e * n_seq + s) * nj + j
    total = n_exp * n_seq * nj
    slot = k % 2

    def row_pairs(kk, sl):
        jj = kk % nj
        ss = (kk // nj) % n_seq
        ee = kk // (nj * n_seq)
        base = (ss * n_exp + ee) * capp + jj * tr

        def one(r):
            tok = idx_ref[base + r]
            return o_hbm.at[pl.ds(ss * seq_len + tok, 1), :], xbuf.at[sl, pl.ds(r, 1), :]

        return one

    def gather_start(kk, sl):
        one = row_pairs(kk, sl)
        for r in range(tr):
            src, dst = one(r)
            pltpu.make_async_copy(src, dst, gsem.at[sl]).start()

    def scatter_start(kk, sl):
        one = row_pairs(kk, sl)
        for r in range(tr):
            dst, src = one(r)
            pltpu.make_async_copy(src, dst, ssem.at[sl]).start()

    def gather_wait(sl):
        pltpu.make_async_copy(o_hbm.at[pl.ds(0, tr), :], xbuf.at[sl], gsem.at[sl]).wait()

    def scatter_wait(sl):
        pltpu.make_async_copy(xbuf.at[sl], o_hbm.at[pl.ds(0, tr), :], ssem.at[sl]).wait()

    @pl.when(k == 0)
    def _():
        gather_start(k, slot)

    gather_wait(slot)

    @pl.when(k + 1 < total)
    def _():
        @pl.when(k >= 1)
        def _():
            scatter_wait(1 - slot)

        gather_start(k + 1, 1 - slot)

    xbuf[slot] = xbuf[slot] + g_ref[0] * y_ref[0].astype(F32)
    scatter_start(k, slot)

    @pl.when(k == total - 1)
    def _():
        scatter_wait(slot)

        @pl.when(total > 1)
        def _():
            scatter_wait(1 - slot)


def _combine(x2d, y, idx, g2, seq_len, cap, row0):
    t, d = x2d.shape
    s, e, _, capp = idx.shape
    tr = _tile(cap, 512, 16)
    nj = cap // tr
    assert s * nj >= 2, "consecutive steps must touch disjoint rows"
    assert row0 % tr == 0 and t >= tr
    nrows = g2.shape[0]
    grid_spec = pltpu.PrefetchScalarGridSpec(
        num_scalar_prefetch=1,
        grid=(e, s, nj),
        in_specs=[
            pl.BlockSpec((1, tr, d), lambda ei, si, j, idx_r: (ei, row0 // tr + si * nj + j, 0)),
            pl.BlockSpec((1, 1, d), lambda ei, si, j, idx_r: (si if nrows > 1 else 0, 0, 0)),
            pl.BlockSpec(memory_space=pl.ANY),
        ],
        out_specs=pl.BlockSpec(memory_space=pl.ANY),
        scratch_shapes=[pltpu.VMEM((2, tr, d), F32), pltpu.SemaphoreType.DMA((2,)), pltpu.SemaphoreType.DMA((2,))],
    )
    return pl.pallas_call(
        functools.partial(_combine_kernel, seq_len=seq_len, capp=capp, tr=tr, n_seq=s, n_exp=e),
        out_shape=jax.ShapeDtypeStruct((t, d), F32),
        grid_spec=grid_spec,
        input_output_aliases={3: 0},
        compiler_params=_cparams(("arbitrary", "arbitrary", "arbitrary")),
        name="moe_combine",
    )(idx.reshape(-1), y, g2, x2d)


def _moe(srcs, gain, wr_hi, wr_lo, w_gate, w_up, w_down, layer):
    n_exp = wr_hi.shape[0]
    routed = []
    for x2d, shift, scale, _, seq_len in srcs:
        cap = CAPACITY_FACTOR * seq_len // n_exp
        hp, aff = _norm_router(x2d, shift, scale, gain, wr_hi, wr_lo, seq_len)
        lpad = -(-seq_len // (LANES * SUBLANES)) * (LANES * SUBLANES)
        if lpad != seq_len:
            aff = jnp.pad(aff, ((0, 0), (0, 0), (0, lpad - seq_len)), constant_values=-1.0)
        idx, gate = _topk(aff, cap)
        routed.append((hp, idx, gate, seq_len, cap))
    y = _expert_ffn(routed, w_gate, w_up, w_down, layer)
    outs, row0 = [], 0
    for (x2d, _, _, g2, seq_len), (_, idx, _, _, cap) in zip(srcs, routed):
        outs.append(_combine(x2d, y, idx, g2, seq_len, cap, row0))
        row0 += idx.shape[0] * cap
    return outs


def kernel(x, c, ctx, c_ctx, w_ada, b_ada, norm1, norm2, w_in, q_norm, k_norm, rpb, conv_w, sg_norm, sg_w,
           sg_b, w_out, w_router, w_gate, w_up, w_down):
    b, n, d = x.shape
    lc = ctx.shape[1]
    depth = w_ada.shape[0]
    na = d // 2
    cwid = d // 4
    swid = d - na - cwid
    in_cols = w_in.shape[2]
    assert in_cols == 3 * na + 3 * cwid + 2 * swid and b + 1 <= SUBLANES

    cvec = jnp.concatenate([c, c_ctx[None, :], jnp.zeros((SUBLANES - b - 1, d), F32)], axis=0)
    mod = _adaln(cvec, w_ada, b_ada).reshape(depth, SUBLANES, N_MOD, 1, d)

    w_in_bf = _cast_bf16(w_in)
    w_out_bf = _cast_bf16(w_out)
    xl = x.reshape(b * n, d)
    xc = ctx.reshape(b * lc, d)
    for l in range(depth):
        update_ctx = l < depth - 1
        lat = [mod[l, :b, k] for k in range(N_MOD)]
        cx = [mod[l, b:b + 1, k] for k in range(N_MOD)]
        sg_w_bf = sg_w[l].astype(BF16)
        wr_t = w_router[l].T
        wr_hi = wr_t.astype(BF16)
        wr_lo = (wr_t - wr_hi.astype(F32)).astype(BF16)
        bias = _bias_table(rpb[l], n // GRID_W)

        c_col0, c_ncols = (0, in_cols) if update_ctx else (na, 2 * na)
        pc = _inproj(xc, cx[0], cx[1], norm1[l], w_in_bf, l, q_norm[l], k_norm[l], lc,
                     col0=c_col0, ncols=c_ncols, na_width=na).reshape(b, lc, c_ncols)
        p = _inproj(xl, lat[0], lat[1], norm1[l], w_in_bf, l, q_norm[l], k_norm[l], n,
                    col0=0, ncols=in_cols, na_width=na).reshape(b, n, in_cols)

        a = _natten(p, pc, bias, na_width=na, kc_col=na - c_col0, vc_col=2 * na - c_col0)
        m = _mix(p, conv_w[l], sg_norm[l], sg_w_bf, sg_b[l], col0=3 * na, cwid=cwid, swid=swid)
        xl = _outproj(a.reshape(b * n, na), m.reshape(b * n, cwid + swid), w_out_bf, l, xl, lat[2], n)
        srcs = [(xl, lat[3], lat[4], lat[5], n)]
        if update_ctx:
            a_c = _ctx_attn(pc, na_width=na)
            m_c = _mix(pc, conv_w[l], sg_norm[l], sg_w_bf, sg_b[l], col0=3 * na, cwid=cwid, swid=swid)
            xc = _outproj(a_c.reshape(b * lc, na), m_c.reshape(b * lc, cwid + swid), w_out_bf, l, xc, cx[2], lc)
            srcs.append((xc, cx[3], cx[4], cx[5], lc))
        outs = _moe(srcs, norm2[l], wr_hi, wr_lo, w_gate, w_up, w_down, l)
        xl = outs[0]
        if update_ctx:
            xc = outs[1]
    return xl.reshape(b, n, d)
```
